```python
import math, functools
import jax, jax.numpy as jnp
from jax import lax
import numpy as np

D_MODEL = 1024
BATCH = 2
SEQ = 8192
DEPTH = 2
DEC_BATCH = 32
DEC_SEQ = 8
PAST_LEN = 8192
PAGE_SIZE = 128

SSM_EXPAND = 2
SSM_D_INNER = SSM_EXPAND * D_MODEL
SSM_HEAD_DIM = 64
SSM_HEADS = SSM_D_INNER // SSM_HEAD_DIM
SSM_GROUPS = 4
SSM_D_STATE = 128
SSM_CONV_W = 4
SSM_CONV_DIM = SSM_D_INNER + 2 * SSM_GROUPS * SSM_D_STATE
SSM_CHUNK = 128
FOX_HEAD_DIM = 128
FOX_HEADS = D_MODEL // FOX_HEAD_DIM
FOX_WIDTH = FOX_HEADS * FOX_HEAD_DIM
FOX_BLOCK = 128
FOX_SCALE = FOX_HEAD_DIM ** -0.5
PEER_HEADS = 8
PEER_N_KEYS = 128
PEER_N_EXPERTS = PEER_N_KEYS * PEER_N_KEYS
PEER_TOPK = 16
PEER_D_KEY = 256
PEER_BLOCK = 128
PLE_DIM = 256
DN_ALPHA = (2.0 * DEPTH) ** 0.25
DN_BETA = (8.0 * DEPTH) ** -0.25
LN_EPS = 1e-5
RMS_EPS = 1e-6
IN_SIZES = (SSM_D_INNER, SSM_CONV_DIM, SSM_HEADS, FOX_WIDTH, FOX_WIDTH, FOX_WIDTH, FOX_HEADS, D_MODEL, D_MODEL)
IN_OFFSETS = tuple(int(o) for o in np.cumsum(IN_SIZES)[:-1])
IN_WIDTH = int(sum(IN_SIZES))

kernel_name = "hybrid_ssd_fox_peer_decoder_step"


def layer_norm(x, g, b):
    xf = x.astype(jnp.float32)
    mu = jnp.mean(xf, axis=-1, keepdims=True)
    var = jnp.mean(jnp.square(xf - mu), axis=-1, keepdims=True)
    return ((xf - mu) * lax.rsqrt(var + LN_EPS) * g + b).astype(x.dtype)


def causal_dwconv(xbc, buf, w, b):
    xp = jnp.concatenate([buf.astype(xbc.dtype), xbc], axis=1)
    y = lax.conv_general_dilated(xp, w[:, None, :].astype(xbc.dtype), window_strides=(1,), padding='VALID',
                                 dimension_numbers=('NWC', 'WIO', 'NWC'), feature_group_count=xbc.shape[-1])
    return y + b, xp[:, xp.shape[1] - (SSM_CONV_W - 1):]


def ssd_scan(xh, dt, a_neg, bmat, cmat, h0):
    f32 = jnp.float32
    bsz, L = xh.shape[:2]
    Q = SSM_CHUNK if L % SSM_CHUNK == 0 else L
    nc = L // Q
    G, R, P, N = SSM_GROUPS, SSM_HEADS // SSM_GROUPS, SSM_HEAD_DIM, SSM_D_STATE
    x = (xh.astype(f32) * dt[..., None]).reshape(bsz, nc, Q, G, R, P)
    a_cum = jnp.cumsum((dt * a_neg).reshape(bsz, nc, Q, G, R), axis=2)
    Bm = bmat.astype(f32).reshape(bsz, nc, Q, G, N)
    Cm = cmat.astype(f32).reshape(bsz, nc, Q, G, N)
    causal = jnp.tril(jnp.ones((Q, Q), dtype=bool))[:, :, None, None]
    seg = a_cum[:, :, :, None] - a_cum[:, :, None, :]
    decay = jnp.exp(jnp.where(causal, seg, -jnp.inf))
    cb = jnp.einsum('bclgn,bcsgn->bclsg', Cm, Bm)
    y_diag = jnp.einsum('bclsg,bclsgr,bcsgrp->bclgrp', cb, decay, x)
    decay_end = jnp.exp(a_cum[:, :, -1:] - a_cum)
    chunk_states = jnp.einsum('bcsgn,bcsgr,bcsgrp->bcgrpn', Bm, decay_end, x)
    chunk_decay = jnp.exp(a_cum[:, :, -1])

    def step(h, inp):
        dec, st = inp
        return dec[..., None, None] * h + st, h

    h_final, h_prev = lax.scan(step, h0.astype(f32).reshape(bsz, G, R, P, N),
                               (jnp.moveaxis(chunk_decay, 1, 0), jnp.moveaxis(chunk_states, 1, 0)))
    h_prev = jnp.moveaxis(h_prev, 0, 1)
    y_off = jnp.einsum('bclgn,bclgr,bcgrpn->bclgrp', Cm, jnp.exp(a_cum), h_prev)
    y = (y_diag + y_off).reshape(bsz, L, SSM_HEADS, P)
    return y, h_final.reshape(bsz, SSM_HEADS, P, N)


def mamba_branch(z, xbc, dt_raw, conv_buf, h0, conv_w, conv_b, dt_bias, a_log, d_skip, norm_w):
    f32 = jnp.float32
    bsz, L = z.shape[:2]
    xbc_c, new_buf = causal_dwconv(xbc, conv_buf, conv_w, conv_b)
    xbc_c = jax.nn.silu(xbc_c)
    xs, bm, cm = jnp.split(xbc_c, [SSM_D_INNER, SSM_D_INNER + SSM_GROUPS * SSM_D_STATE], axis=-1)
    xh = xs.reshape(bsz, L, SSM_HEADS, SSM_HEAD_DIM)
    dt = jax.nn.softplus(dt_raw.astype(f32) + dt_bias)
    a_neg = -jnp.exp(a_log.astype(f32))
    y, h_final = ssd_scan(xh, dt, a_neg, bm.reshape(bsz, L, SSM_GROUPS, SSM_D_STATE),
                          cm.reshape(bsz, L, SSM_GROUPS, SSM_D_STATE), h0)
    y = y + d_skip[:, None] * xh.astype(f32)
    y = y.reshape(bsz, L, SSM_D_INNER) * jax.nn.silu(z.astype(f32))
    yg = y.reshape(bsz, L, SSM_GROUPS, SSM_D_INNER // SSM_GROUPS)
    yg = yg * lax.rsqrt(jnp.mean(yg * yg, axis=-1, keepdims=True) + RMS_EPS)
    y = yg.reshape(bsz, L, SSM_D_INNER) * norm_w
    return y.astype(z.dtype), h_final, new_buf


def fox_attend(q, k, v, cq, ck, qpos, kpos):
    s = jnp.einsum('bqhd,bkhd->bhqk', q, k).astype(jnp.float32) * FOX_SCALE
    bias = jnp.swapaxes(cq, 1, 2)[:, :, :, None] - jnp.swapaxes(ck, 1, 2)[:, :, None, :]
    s = jnp.where(kpos[None, :] <= qpos[:, None], s + bias, -jnp.inf)
    p = jax.nn.softmax(s, axis=-1)
    return jnp.einsum('bhqk,bkhd->bqhd', p.astype(v.dtype), v)


def fox_prompt(q, k, v, logf):
    bsz, L = q.shape[:2]
    c = jnp.cumsum(logf.astype(jnp.float32), axis=1)
    pos = jnp.arange(L)

    def blk(i):
        s0 = i * FOX_BLOCK
        qb = lax.dynamic_slice_in_dim(q, s0, FOX_BLOCK, axis=1)
        cb = lax.dynamic_slice_in_dim(c, s0, FOX_BLOCK, axis=1)
        pb = lax.dynamic_slice_in_dim(pos, s0, FOX_BLOCK)
        return fox_attend(qb, k, v, cb, c, pb, pos)

    o = lax.map(blk, jnp.arange(L // FOX_BLOCK))
    return jnp.moveaxis(o, 0, 1).reshape(bsz, L, FOX_HEADS, FOX_HEAD_DIM)


def fox_sample(q, k, v, logf, k_pages, v_pages, lf_pages, page_table):
    bsz, L = q.shape[:2]
    past = page_table.shape[1] * k_pages.shape[1]
    k_past = k_pages[page_table].reshape(bsz, past, FOX_HEADS, FOX_HEAD_DIM).astype(k.dtype)
    v_past = v_pages[page_table].reshape(bsz, past, FOX_HEADS, FOX_HEAD_DIM).astype(v.dtype)
    lf_past = lf_pages[page_table].reshape(bsz, past, FOX_HEADS).astype(jnp.float32)
    k_all = jnp.concatenate([k_past, k], axis=1)
    v_all = jnp.concatenate([v_past, v], axis=1)
    c = jnp.cumsum(jnp.concatenate([lf_past, logf.astype(jnp.float32)], axis=1), axis=1)
    kpos = jnp.arange(past + L)
    qpos = past + jnp.arange(L)
    return fox_attend(q, k_all, v_all, c[:, past:], c, qpos, kpos)


def peer_ffn(x, w_q, sub_keys, u_tab, v_tab):
    T = x.shape[0]
    blk = PEER_BLOCK if T % PEER_BLOCK == 0 else T
    half = PEER_D_KEY // 2
    k1 = sub_keys[0].astype(jnp.float32)
    k2 = sub_keys[1].astype(jnp.float32)

    def one(xb):
        q = (xb @ w_q).reshape(blk, PEER_HEADS, 2, half).astype(jnp.float32)
        s1 = jnp.einsum('thd,hkd->thk', q[:, :, 0], k1)
        s2 = jnp.einsum('thd,hkd->thk', q[:, :, 1], k2)
        v1, i1 = lax.top_k(s1, PEER_TOPK)
        v2, i2 = lax.top_k(s2, PEER_TOPK)
        cand = (v1[..., :, None] + v2[..., None, :]).reshape(blk, PEER_HEADS, PEER_TOPK * PEER_TOPK)
        cidx = (i1[..., :, None] * PEER_N_KEYS + i2[..., None, :]).reshape(blk, PEER_HEADS, PEER_TOPK * PEER_TOPK)
        sc, sel = lax.top_k(cand, PEER_TOPK)
        idx = jnp.take_along_axis(cidx, sel, axis=-1)
        g = jax.nn.softmax(sc, axis=-1)
        hid = jax.nn.gelu(jnp.einsum('td,thkd->thk', xb, u_tab[idx]).astype(jnp.float32), approximate=False)
        coef = (g * hid).astype(xb.dtype)
        return jnp.einsum('thk,thkd->td', coef, v_tab[idx])

    return lax.map(one, x.reshape(T // blk, blk, x.shape[-1])).reshape(T, x.shape[-1])


def trunk_layer(x, p_emb, conv_buf, h0, attend, w_in, conv_w, conv_b, dt_bias, a_log, d_skip,
                ssm_norm_w, fox_f_bias, w_branch_a, w_branch_b, w_out, ln1_g, ln1_b, ln2_g, ln2_b,
                peer_w_q, peer_sub_keys, peer_u, peer_v, w_ple_gate, w_ple_proj):
    bsz, L, _ = x.shape
    proj = x @ w_in
    z, xbc, dt_raw, q, k, v, f_raw, gate_a, gate_b = jnp.split(proj, IN_OFFSETS, axis=-1)
    y_a, h_fin, new_buf = mamba_branch(z, xbc, dt_raw, conv_buf, h0, conv_w, conv_b,
                                       dt_bias, a_log, d_skip, ssm_norm_w)
    hshape = (bsz, L, FOX_HEADS, FOX_HEAD_DIM)
    q, k, v = q.reshape(hshape), k.reshape(hshape), v.reshape(hshape)
    logf = jax.nn.log_sigmoid(f_raw.astype(jnp.float32) + fox_f_bias)
    y_b = attend(q, k, v, logf).reshape(bsz, L, FOX_WIDTH)
    merged = jax.nn.sigmoid(gate_a) * (y_a @ w_branch_a) + jax.nn.sigmoid(gate_b) * (y_b @ w_branch_b)
    x = layer_norm(DN_ALPHA * x + merged @ w_out, ln1_g, ln1_b)
    ffn = peer_ffn(x.reshape(bsz * L, D_MODEL), peer_w_q, peer_sub_keys, peer_u, peer_v).reshape(x.shape)
    x = layer_norm(DN_ALPHA * x + ffn, ln2_g, ln2_b)
    x = x + jax.nn.sigmoid(x @ w_ple_gate) * (p_emb @ w_ple_proj)
    return x, k, v, logf, h_fin.astype(x.dtype), new_buf


def setup_inputs(seed: int = 0) -> dict:
    key = jax.random.key(seed)
    ks = iter(jax.random.split(key, 48))

    def nrm(shape, scale):
        return jax.random.normal(next(ks), shape, jnp.float32) * scale

    n_pages = PAST_LEN // PAGE_SIZE
    n_used = DEC_BATCH * n_pages
    n_pool = n_used + n_used // 4
    x_prompt = nrm((BATCH, SEQ, D_MODEL), 1.0)
    x_sample = nrm((DEC_BATCH, DEC_SEQ, D_MODEL), 1.0)
    cache_k = nrm((DEPTH, n_pool, PAGE_SIZE, FOX_HEADS, FOX_HEAD_DIM), 1.0)
    cache_v = nrm((DEPTH, n_pool, PAGE_SIZE, FOX_HEADS, FOX_HEAD_DIM), 1.0)
    cache_logf = jax.nn.log_sigmoid(3.0 + nrm((DEPTH, n_pool, PAGE_SIZE, FOX_HEADS), 1.0))
    state_ssm = nrm((DEPTH, DEC_BATCH, SSM_HEADS, SSM_HEAD_DIM, SSM_D_STATE), 0.1)
    state_conv = nrm((DEPTH, DEC_BATCH, SSM_CONV_W - 1, SSM_CONV_DIM), 1.0)
    page_table = jax.random.permutation(next(ks), n_pool)[:n_used].reshape(DEC_BATCH, n_pages).astype(jnp.int32)
    p_prompt = nrm((DEPTH, BATCH, SEQ, PLE_DIM), 1.0)
    p_sample = nrm((DEPTH, DEC_BATCH, DEC_SEQ, PLE_DIM), 1.0)
    w_in = nrm((DEPTH, D_MODEL, IN_WIDTH), D_MODEL ** -0.5)
    conv_w = nrm((DEPTH, SSM_CONV_W, SSM_CONV_DIM), SSM_CONV_W ** -0.5)
    conv_b = nrm((DEPTH, SSM_CONV_DIM), 0.02)
    dt0 = jnp.exp(jax.random.uniform(next(ks), (DEPTH, SSM_HEADS), jnp.float32,
                                     minval=math.log(1e-3), maxval=math.log(1e-1)))
    dt_bias = dt0 + jnp.log(-jnp.expm1(-dt0))
    a_log = jnp.log(jax.random.uniform(next(ks), (DEPTH, SSM_HEADS), jnp.float32, minval=1.0, maxval=16.0))
    d_skip = 1.0 + nrm((DEPTH, SSM_HEADS), 0.1)
    ssm_norm_w = 1.0 + nrm((DEPTH, SSM_D_INNER), 0.02)
    fox_f_bias = jax.random.uniform(next(ks), (DEPTH, FOX_HEADS), jnp.float32, minval=1.0, maxval=6.0)
    w_branch_a = nrm((DEPTH, SSM_D_INNER, D_MODEL), SSM_D_INNER ** -0.5)
    w_branch_b = nrm((DEPTH, FOX_WIDTH, D_MODEL), FOX_WIDTH ** -0.5)
    w_out = nrm((DEPTH, D_MODEL, D_MODEL), DN_BETA * D_MODEL ** -0.5)
    ln1_g = 1.0 + nrm((DEPTH, D_MODEL), 0.02)
    ln1_b = nrm((DEPTH, D_MODEL), 0.02)
    ln2_g = 1.0 + nrm((DEPTH, D_MODEL), 0.02)
    ln2_b = nrm((DEPTH, D_MODEL), 0.02)
    peer_w_q = nrm((DEPTH, D_MODEL, PEER_HEADS * PEER_D_KEY), D_MODEL ** -0.5)
    peer_sub_keys = nrm((DEPTH, 2, PEER_HEADS, PEER_N_KEYS, PEER_D_KEY // 2), (PEER_D_KEY // 2) ** -0.5)
    peer_u = nrm((DEPTH, PEER_N_EXPERTS, D_MODEL), D_MODEL ** -0.5)
    peer_v = nrm((DEPTH, PEER_N_EXPERTS, D_MODEL), DN_BETA * PEER_HEADS ** -0.5)
    w_ple_gate = nrm((DEPTH, D_MODEL, D_MODEL), D_MODEL ** -0.5)
    w_ple_proj = nrm((DEPTH, PLE_DIM, D_MODEL), PLE_DIM ** -0.5)
    return {"x_prompt": x_prompt, "x_sample": x_sample, "cache_k": cache_k, "cache_v": cache_v,
            "cache_logf": cache_logf, "state_ssm": state_ssm, "state_conv": state_conv,
            "page_table": page_table, "p_prompt": p_prompt, "p_sample": p_sample,
            "w_in": w_in, "conv_w": conv_w, "conv_b": conv_b, "dt_bias": dt_bias, "a_log": a_log,
            "d_skip": d_skip, "ssm_norm_w": ssm_norm_w, "fox_f_bias": fox_f_bias,
            "w_branch_a": w_branch_a, "w_branch_b": w_branch_b, "w_out": w_out,
            "ln1_g": ln1_g, "ln1_b": ln1_b, "ln2_g": ln2_g, "ln2_b": ln2_b,
            "peer_w_q": peer_w_q, "peer_sub_keys": peer_sub_keys, "peer_u": peer_u, "peer_v": peer_v,
            "w_ple_gate": w_ple_gate, "w_ple_proj": w_ple_proj}


def reference(x_prompt, x_sample, cache_k, cache_v, cache_logf, state_ssm, state_conv, page_table,
              p_prompt, p_sample, w_in, conv_w, conv_b, dt_bias, a_log, d_skip, ssm_norm_w, fox_f_bias,
              w_branch_a, w_branch_b, w_out, ln1_g, ln1_b, ln2_g, ln2_b, peer_w_q, peer_sub_keys,
              peer_u, peer_v, w_ple_gate, w_ple_proj):
    bp = x_prompt.shape[0]
    xp, xs = x_prompt, x_sample
    kp, vp, lfp, hp, cp = [], [], [], [], []
    ksm, vsm, lfs, hs, cs = [], [], [], [], []
    for i in range(DEPTH):
        lw = (w_in[i], conv_w[i], conv_b[i], dt_bias[i], a_log[i], d_skip[i], ssm_norm_w[i], fox_f_bias[i],
              w_branch_a[i], w_branch_b[i], w_out[i], ln1_g[i], ln1_b[i], ln2_g[i], ln2_b[i],
              peer_w_q[i], peer_sub_keys[i], peer_u[i], peer_v[i], w_ple_gate[i], w_ple_proj[i])
        buf0 = jnp.zeros((bp, SSM_CONV_W - 1, SSM_CONV_DIM), xp.dtype)
        h00 = jnp.zeros((bp, SSM_HEADS, SSM_HEAD_DIM, SSM_D_STATE), xp.dtype)
        xp, k_, v_, lf_, h_, c_ = trunk_layer(xp, p_prompt[i], buf0, h00, fox_prompt, *lw)
        kp.append(k_); vp.append(v_); lfp.append(lf_); hp.append(h_); cp.append(c_)
        attend_s = functools.partial(fox_sample, k_pages=cache_k[i], v_pages=cache_v[i],
                                     lf_pages=cache_logf[i], page_table=page_table)
        xs, k_, v_, lf_, h_, c_ = trunk_layer(xs, p_sample[i], state_conv[i], state_ssm[i], attend_s, *lw)
        ksm.append(k_); vsm.append(v_); lfs.append(lf_); hs.append(h_); cs.append(c_)
    return (xp, xs,
            jnp.stack(kp), jnp.stack(vp), jnp.stack(lfp), jnp.stack(hp), jnp.stack(cp),
            jnp.stack(ksm), jnp.stack(vsm), jnp.stack(lfs), jnp.stack(hs), jnp.stack(cs))
```

```python
import functools
import math

import numpy as np
import jax
import jax.numpy as jnp
from jax import lax
from jax.experimental import pallas as pl
from jax.experimental.pallas import tpu as pltpu

F32 = jnp.float32
BF16 = jnp.bfloat16
HIGHEST = lax.Precision.HIGHEST

D_MODEL = 1024
SSM_D_INNER = 2048
SSM_HEAD_DIM = 64
SSM_HEADS = 32
SSM_GROUPS = 4
SSM_D_STATE = 128
SSM_CONV_W = 4
SSM_CONV_DIM = SSM_D_INNER + 2 * SSM_GROUPS * SSM_D_STATE
SSM_CHUNK = 128
FOX_HEAD_DIM = 128
FOX_HEADS = 8
FOX_WIDTH = FOX_HEADS * FOX_HEAD_DIM
FOX_SCALE = FOX_HEAD_DIM ** -0.5
PEER_HEADS = 8
PEER_N_KEYS = 128
PEER_TOPK = 16
PEER_HALF = 128
LN_EPS = 1e-5
RMS_EPS = 1e-6

LANES = 128
SUBLANES = 8
VMEM_LIMIT_BYTES = 56 * 1024 * 1024

DT_LO, DT_HI = 0, SSM_HEADS
F_LO, F_HI = SSM_HEADS, SSM_HEADS + FOX_HEADS
NOT_RANKED = 99.0


def _pick_block(n, target, mult=SUBLANES):
    for d in range(min(n, target), 0, -1):
        if n % d == 0 and d % mult == 0:
            return d
    return n


def _params(sem):
    return pltpu.CompilerParams(dimension_semantics=sem, vmem_limit_bytes=VMEM_LIMIT_BYTES)


def _mm_body(x_ref, w_ref, o_ref, *, scale):
    acc = jnp.dot(x_ref[...].astype(BF16), w_ref[...], preferred_element_type=F32)
    if scale is not None:
        acc = acc * scale
    o_ref[...] = acc.astype(o_ref.dtype)


def _matmul(x, w, out_dtype=F32, scale=None, tm_target=1280, tn_target=512):
    m, k = x.shape
    n = w.shape[1]
    tm = _pick_block(m, tm_target)
    tn = _pick_block(n, tn_target, LANES)
    return pl.pallas_call(
        functools.partial(_mm_body, scale=scale),
        grid=(m // tm, n // tn),
        in_specs=[pl.BlockSpec((tm, k), lambda i, j: (i, 0)),
                  pl.BlockSpec((k, tn), lambda i, j: (0, j))],
        out_specs=pl.BlockSpec((tm, tn), lambda i, j: (i, j)),
        out_shape=jax.ShapeDtypeStruct((m, n), out_dtype),
        compiler_params=_params(("parallel", "parallel")),
        name="proj_matmul",
    )(x, w)


def _ssd_body(xbc_ref, z_ref, sm_ref, cst_ref, h0_ref, cw_ref, cb_ref, bias_ref, aneg_ref,
              dsk_ref, nw_ref, e_ref, tri_ref,
              y_ref, lf_ref, c_ref, hfin_ref,
              xp_s, ht_s, cc_s, y_s, *, rows, nc):
    q = SSM_CHUNK
    ci = pl.program_id(1)

    @pl.when(ci == 0)
    def _():
        xp_s[0:SUBLANES, :] = cst_ref[0]
        ht_s[...] = h0_ref[0].T
        cc_s[...] = jnp.zeros_like(cc_s)

    if rows < q:
        pad = q - rows
        xp_s[SUBLANES:SUBLANES + rows, :] = xbc_ref[...]
        xp_s[SUBLANES + rows:SUBLANES + q, :] = jnp.zeros((pad, SSM_CONV_DIM), F32)
        z = jnp.concatenate([z_ref[...], jnp.zeros((pad, SSM_D_INNER), F32)], axis=0)
        sm = jnp.concatenate([sm_ref[...], jnp.zeros((pad, LANES), F32)], axis=0)
    else:
        xp_s[SUBLANES:SUBLANES + q, :] = xbc_ref[...]
        z = z_ref[...]
        sm = sm_ref[...]

    w = cw_ref[...]
    base = SUBLANES - (SSM_CONV_W - 1)
    conv = cb_ref[...]
    for t in range(SSM_CONV_W):
        conv = conv + xp_s[base + t:base + t + q, :] * w[t:t + 1, :]
    act = jax.nn.silu(conv)
    xp_s[0:SUBLANES, :] = xp_s[q:q + SUBLANES, :]

    lane = lax.broadcasted_iota(jnp.int32, (q, LANES), 1)
    row = lax.broadcasted_iota(jnp.int32, (q, LANES), 0)
    valid = row < rows
    sm = sm + bias_ref[...]
    dt = jnp.where(valid, jax.nn.softplus(sm), 0.0)
    logf = jnp.where(valid, jax.nn.log_sigmoid(sm), 0.0)
    is_dt = lane < DT_HI
    is_f = (lane >= F_LO) & (lane < F_HI)
    steps = jnp.where(is_dt, dt * aneg_ref[...], jnp.where(is_f, logf, 0.0))
    cum = jnp.dot(tri_ref[...], steps, precision=HIGHEST, preferred_element_type=F32)
    cfull = cum + cc_s[...]
    cc_s[...] = jnp.where(is_f[0:1, :], cfull[q - 1:q, :], 0.0)
    lf_ref[...] = logf[0:rows, :]
    c_ref[...] = cfull[0:rows, :]

    acum = cum
    ea = jnp.exp(acum)
    de = jnp.exp(acum[q - 1:q, :] - acum)
    e = e_ref[...]
    dt_e = jnp.dot(dt, e, precision=HIGHEST, preferred_element_type=F32)
    ea_e = jnp.dot(ea, e, precision=HIGHEST, preferred_element_type=F32)
    de_e = jnp.dot(de, e, precision=HIGHEST, preferred_element_type=F32)
    cd_e = ea_e[q - 1:q, :]

    xs = act[:, 0:SSM_D_INNER]
    xd = xs * dt_e
    acum_t = acum.T
    li = lax.broadcasted_iota(jnp.int32, (q, q), 0)
    si = lax.broadcasted_iota(jnp.int32, (q, q), 1)
    causal = li >= si
    half_lane = lax.broadcasted_iota(jnp.int32, (q, LANES), 1) < SSM_HEAD_DIM
    gw = SSM_D_INNER // SSM_GROUPS
    hpg = SSM_HEADS // SSM_GROUPS
    for g in range(SSM_GROUPS):
        bg = act[:, SSM_D_INNER + g * SSM_D_STATE:SSM_D_INNER + (g + 1) * SSM_D_STATE]
        cg = act[:, SSM_D_INNER + (SSM_GROUPS + g) * SSM_D_STATE:SSM_D_INNER + (SSM_GROUPS + g + 1) * SSM_D_STATE]
        bg16 = bg.astype(BF16)
        cg16 = cg.astype(BF16)
        cb = lax.dot_general(cg16, bg16, (((1,), (1,)), ((), ())), preferred_element_type=F32)
        for pr in range(hpg // 2):
            h0 = g * hpg + 2 * pr
            ms = []
            for h in (h0, h0 + 1):
                seg = acum[:, h:h + 1] - acum_t[h:h + 1, :]
                dec = jnp.exp(jnp.where(causal, seg, -jnp.inf))
                ms.append((cb * dec).astype(BF16))
            lhs = jnp.concatenate(ms, axis=1)
            xp = xd[:, h0 * SSM_HEAD_DIM:(h0 + 2) * SSM_HEAD_DIM]
            rhs = jnp.concatenate([jnp.where(half_lane, xp, 0.0), jnp.where(half_lane, 0.0, xp)],
                                  axis=0).astype(BF16)
            y_s[:, h0 * SSM_HEAD_DIM:(h0 + 2) * SSM_HEAD_DIM] = jnp.dot(
                lhs, rhs, preferred_element_type=F32)
        sl = slice(g * gw, (g + 1) * gw)
        ht_g = ht_s[:, sl]
        y_off = jnp.dot(cg16, ht_g.astype(BF16), preferred_element_type=F32) * ea_e[:, sl]
        y_s[:, sl] = y_s[:, sl] + y_off
        st = jnp.dot(bg.T.astype(BF16), (xd[:, sl] * de_e[:, sl]).astype(BF16), preferred_element_type=F32)
        ht_s[:, sl] = ht_g * cd_e[:, sl] + st

    y = y_s[...] + dsk_ref[...] * xs
    y = y * jax.nn.silu(z)
    nw = nw_ref[...]
    for g in range(SSM_GROUPS):
        sl = slice(g * gw, (g + 1) * gw)
        yg = y[:, sl]
        ms = jnp.sum(yg * yg, axis=1, keepdims=True) * (1.0 / gw)
        y_ref[:, sl] = (yg * lax.rsqrt(ms + RMS_EPS) * nw[:, sl])[0:rows, :].astype(y_ref.dtype)

    @pl.when(ci == nc - 1)
    def _():
        hfin_ref[0] = ht_s[...].T


def _ssd(xbc, z, small, conv_state8, h0, lw, *, row0, batch, seqlen):
    q = SSM_CHUNK
    rows = q if seqlen % q == 0 else seqlen
    assert rows % SUBLANES == 0 and rows <= q and row0 % rows == 0
    nc = seqlen // rows
    blk0 = row0 // rows
    tok = lambda b, c: (blk0 + b * nc + c, 0)
    out_tok = lambda b, c: (b * nc + c, 0)
    const2 = lambda b, c: (0, 0)
    n = batch * seqlen
    return pl.pallas_call(
        functools.partial(_ssd_body, rows=rows, nc=nc),
        grid=(batch, nc),
        in_specs=[pl.BlockSpec((rows, SSM_CONV_DIM), tok),
                  pl.BlockSpec((rows, SSM_D_INNER), tok),
                  pl.BlockSpec((rows, LANES), tok),
                  pl.BlockSpec((1, SUBLANES, SSM_CONV_DIM), lambda b, c: (b, 0, 0)),
                  pl.BlockSpec((1, SSM_D_INNER, SSM_D_STATE), lambda b, c: (b, 0, 0)),
                  pl.BlockSpec((SUBLANES, SSM_CONV_DIM), const2),
                  pl.BlockSpec((1, SSM_CONV_DIM), const2),
                  pl.BlockSpec((1, LANES), const2),
                  pl.BlockSpec((1, LANES), const2),
                  pl.BlockSpec((1, SSM_D_INNER), const2),
                  pl.BlockSpec((1, SSM_D_INNER), const2),
                  pl.BlockSpec((LANES, SSM_D_INNER), const2),
                  pl.BlockSpec((q, q), const2)],
        out_specs=[pl.BlockSpec((rows, SSM_D_INNER), out_tok),
                   pl.BlockSpec((rows, LANES), out_tok),
                   pl.BlockSpec((rows, LANES), out_tok),
                   pl.BlockSpec((1, SSM_D_INNER, SSM_D_STATE), lambda b, c: (b, 0, 0))],
        out_shape=[jax.ShapeDtypeStruct((n, SSM_D_INNER), BF16),
                   jax.ShapeDtypeStruct((n, LANES), F32),
                   jax.ShapeDtypeStruct((n, LANES), F32),
                   jax.ShapeDtypeStruct((batch, SSM_D_INNER, SSM_D_STATE), F32)],
        scratch_shapes=[pltpu.VMEM((q + SUBLANES, SSM_CONV_DIM), F32),
                        pltpu.VMEM((SSM_D_STATE, SSM_D_INNER), F32),
                        pltpu.VMEM((1, LANES), F32),
                        pltpu.VMEM((q, SSM_D_INNER), F32)],
        compiler_params=_params(("parallel", "arbitrary")),
        name="ssd_scan",
    )(xbc, z, small, conv_state8, h0, lw["conv_w8"], lw["conv_b"], lw["bias128"], lw["aneg128"],
      lw["dskip_e"], lw["norm_w"], lw["expand"], lw["tri"])


def _flash_body(q_ref, k_ref, v_ref, cq_ref, ck_ref, o_ref, m_s, l_s, acc_s, *, tq):
    qi = pl.program_id(1)
    ki = pl.program_id(2)

    @pl.when(ki == 0)
    def _():
        m_s[...] = jnp.full_like(m_s, -jnp.inf)
        l_s[...] = jnp.zeros_like(l_s)
        acc_s[...] = jnp.zeros_like(acc_s)

    @pl.when(ki <= qi)
    def _():
        rows = qi * tq + lax.broadcasted_iota(jnp.int32, (tq, tq), 0)
        cols = ki * tq + lax.broadcasted_iota(jnp.int32, (tq, tq), 1)
        causal = cols <= rows
        for h in range(FOX_HEADS):
            sl = slice(h * FOX_HEAD_DIM, (h + 1) * FOX_HEAD_DIM)
            s = lax.dot_general(q_ref[:, sl], k_ref[:, sl].astype(BF16), (((1,), (1,)), ((), ())),
                                preferred_element_type=F32)
            s = s + (cq_ref[:, h:h + 1] - ck_ref[h:h + 1, :])
            s = jnp.where(causal, s, -jnp.inf)
            m_prev = m_s[:, h:h + 1]
            m_new = jnp.maximum(m_prev, jnp.max(s, axis=1, keepdims=True))
            alpha = jnp.exp(m_prev - m_new)
            p = jnp.exp(s - m_new)
            l_s[:, h:h + 1] = alpha * l_s[:, h:h + 1] + jnp.sum(p, axis=1, keepdims=True)
            m_s[:, h:h + 1] = m_new
            acc_s[:, sl] = acc_s[:, sl] * alpha + jnp.dot(p.astype(BF16), v_ref[:, sl].astype(BF16),
                                                        preferred_element_type=F32)

    @pl.when(ki == qi)
    def _():
        for h in range(FOX_HEADS):
            sl = slice(h * FOX_HEAD_DIM, (h + 1) * FOX_HEAD_DIM)
            o_ref[:, sl] = (acc_s[:, sl] / l_s[:, h:h + 1]).astype(o_ref.dtype)


def _fox_prompt(q, k, v, c, c_t, *, batch, seqlen, tq_target=512):
    tq = _pick_block(seqlen, tq_target, LANES)
    nq = seqlen // tq
    return pl.pallas_call(
        functools.partial(_flash_body, tq=tq),
        grid=(batch, nq, nq),
        in_specs=[pl.BlockSpec((tq, FOX_WIDTH), lambda b, i, j: (b * nq + i, 0)),
                  pl.BlockSpec((tq, FOX_WIDTH), lambda b, i, j: (b * nq + jnp.minimum(i, j), 0)),
                  pl.BlockSpec((tq, FOX_WIDTH), lambda b, i, j: (b * nq + jnp.minimum(i, j), 0)),
                  pl.BlockSpec((tq, FOX_HEADS), lambda b, i, j: (b * nq + i, 0)),
                  pl.BlockSpec((FOX_HEADS, tq), lambda b, i, j: (0, b * nq + jnp.minimum(i, j)))],
        out_specs=pl.BlockSpec((tq, FOX_WIDTH), lambda b, i, j: (b * nq + i, 0)),
        out_shape=jax.ShapeDtypeStruct((batch * seqlen, FOX_WIDTH), BF16),
        scratch_shapes=[pltpu.VMEM((tq, LANES), F32), pltpu.VMEM((tq, LANES), F32),
                        pltpu.VMEM((tq, FOX_WIDTH), F32)],
        compiler_params=_params(("parallel", "parallel", "arbitrary")),
        name="fox_prompt",
    )(q, k, v, c, c_t)


def _paged_body(pt_ref, qbd_ref, ccol_ref, knew_ref, vnew_ref, cnew_t_ref, kp_ref, vp_ref, lfp_ref,
                o_ref, m_s, l_s, acc_s, carry_s, *, n_new, n_pages):
    del pt_ref
    step = pl.program_id(1)
    page = kp_ref.shape[2]
    nrow = qbd_ref.shape[1]
    r_i = lax.broadcasted_iota(jnp.int32, (nrow, FOX_HEADS), 0)
    h_i = lax.broadcasted_iota(jnp.int32, (nrow, FOX_HEADS), 1)
    head_of_row = jnp.where((r_i // n_new == h_i) & (r_i < FOX_HEADS * n_new), 1.0, 0.0)

    def update(kblk, vblk, key_bias, mask):
        s = lax.dot_general(qbd_ref[0], kblk.astype(BF16), (((1,), (1,)), ((), ())),
                            preferred_element_type=F32)
        bias = jnp.dot(head_of_row, key_bias, precision=HIGHEST, preferred_element_type=F32)
        s = s + (ccol_ref[0] + bias)
        if mask is not None:
            s = jnp.where(mask, s, -jnp.inf)
        m_prev = m_s[...]
        m_new = jnp.maximum(m_prev, jnp.max(s, axis=1, keepdims=True))
        alpha = jnp.exp(m_prev - m_new)
        p = jnp.exp(s - m_new)
        l_s[...] = alpha * l_s[...] + jnp.sum(p, axis=1, keepdims=True)
        m_s[...] = m_new
        acc_s[...] = acc_s[...] * alpha + jnp.dot(p.astype(BF16), vblk.astype(BF16),
                                                  preferred_element_type=F32)

    @pl.when(step == 0)
    def _():
        m_s[...] = jnp.full_like(m_s, -jnp.inf)
        l_s[...] = jnp.zeros_like(l_s)
        acc_s[...] = jnp.zeros_like(acc_s)
        carry_s[...] = jnp.zeros_like(carry_s)
        rr = lax.broadcasted_iota(jnp.int32, (nrow, page), 0)
        kk = lax.broadcasted_iota(jnp.int32, (nrow, page), 1)
        update(knew_ref[0], vnew_ref[0], -cnew_t_ref[0], kk <= rr % n_new)

    @pl.when(step > 0)
    def _():
        lf_t = lfp_ref[0, 0]
        ii = lax.broadcasted_iota(jnp.int32, (page, page), 0)
        jj = lax.broadcasted_iota(jnp.int32, (page, page), 1)
        later = jnp.where(ii > jj, 1.0, 0.0)
        suffix = jnp.dot(lf_t, later, precision=HIGHEST, preferred_element_type=F32) + carry_s[...]
        carry_s[...] = carry_s[...] + jnp.sum(lf_t, axis=1, keepdims=True)
        update(kp_ref[0, 0], vp_ref[0, 0], suffix, None)

    @pl.when(step == n_pages)
    def _():
        for h in range(FOX_HEADS):
            rs = slice(h * n_new, (h + 1) * n_new)
            cs = slice(h * FOX_HEAD_DIM, (h + 1) * FOX_HEAD_DIM)
            o_ref[0, :, cs] = (acc_s[rs, cs] / l_s[rs, :]).astype(o_ref.dtype)


def _fox_sample(qbd, ccol, knew, vnew, cnew_t, cache_k, cache_v, cache_lf_t, page_table, *, layer, n_new):
    bsz, n_pages = page_table.shape
    page = cache_k.shape[2]
    nrow = qbd.shape[1]
    cache_map = lambda b, s, pt: (layer, pt[b, n_pages - jnp.maximum(s, 1)], 0, 0)
    per_b = lambda b, s, pt: (b, 0, 0)
    grid_spec = pltpu.PrefetchScalarGridSpec(
        num_scalar_prefetch=1,
        grid=(bsz, n_pages + 1),
        in_specs=[pl.BlockSpec((1, nrow, FOX_WIDTH), per_b),
                  pl.BlockSpec((1, nrow, 1), per_b),
                  pl.BlockSpec((1, page, FOX_WIDTH), per_b),
                  pl.BlockSpec((1, page, FOX_WIDTH), per_b),
                  pl.BlockSpec((1, FOX_HEADS, page), per_b),
                  pl.BlockSpec((1, 1, page, FOX_WIDTH), cache_map),
                  pl.BlockSpec((1, 1, page, FOX_WIDTH), cache_map),
                  pl.BlockSpec((1, 1, FOX_HEADS, page), cache_map)],
        out_specs=pl.BlockSpec((1, n_new, FOX_WIDTH), per_b),
        scratch_shapes=[pltpu.VMEM((nrow, 1), F32), pltpu.VMEM((nrow, 1), F32),
                        pltpu.VMEM((nrow, FOX_WIDTH), F32), pltpu.VMEM((FOX_HEADS, 1), F32)])
    return pl.pallas_call(
        functools.partial(_paged_body, n_new=n_new, n_pages=n_pages),
        grid_spec=grid_spec,
        out_shape=jax.ShapeDtypeStruct((bsz, n_new, FOX_WIDTH), BF16),
        compiler_params=_params(("parallel", "arbitrary")),
        name="fox_sample",
    )(page_table, qbd, ccol, knew, vnew, cnew_t, cache_k, cache_v, cache_lf_t)


def _layer_norm(r, g, b):
    mu = jnp.mean(r, axis=1, keepdims=True)
    d = r - mu
    var = jnp.mean(d * d, axis=1, keepdims=True)
    return d * lax.rsqrt(var + LN_EPS) * g + b


def _merge_body(x_ref, ya_ref, yb_ref, ga_ref, gb_ref, wa_ref, wb_ref, wo_ref, g_ref, b_ref, o_ref, *, alpha):
    a = jnp.dot(ya_ref[...], wa_ref[...], preferred_element_type=F32)
    b = jnp.dot(yb_ref[...], wb_ref[...], preferred_element_type=F32)
    merged = jax.nn.sigmoid(ga_ref[...]) * a + jax.nn.sigmoid(gb_ref[...]) * b
    r = alpha * x_ref[...] + jnp.dot(merged.astype(BF16), wo_ref[...], preferred_element_type=F32)
    o_ref[...] = _layer_norm(r, g_ref[...], b_ref[...])


def _merge(x, ya, yb, ga, gb, lw, alpha, tm_target=256):
    m = x.shape[0]
    tm = _pick_block(m, tm_target)
    tok = lambda i: (i, 0)
    const = lambda i: (0, 0)
    return pl.pallas_call(
        functools.partial(_merge_body, alpha=alpha),
        grid=(m // tm,),
        in_specs=[pl.BlockSpec((tm, D_MODEL), tok), pl.BlockSpec((tm, SSM_D_INNER), tok),
                  pl.BlockSpec((tm, FOX_WIDTH), tok), pl.BlockSpec((tm, D_MODEL), tok),
                  pl.BlockSpec((tm, D_MODEL), tok),
                  pl.BlockSpec((SSM_D_INNER, D_MODEL), const), pl.BlockSpec((FOX_WIDTH, D_MODEL), const),
                  pl.BlockSpec((D_MODEL, D_MODEL), const),
                  pl.BlockSpec((1, D_MODEL), const), pl.BlockSpec((1, D_MODEL), const)],
        out_specs=pl.BlockSpec((tm, D_MODEL), tok),
        out_shape=jax.ShapeDtypeStruct((m, D_MODEL), F32),
        compiler_params=_params(("parallel",)),
        name="merge_ln1",
    )(x, ya, yb, ga, gb, lw["w_a"], lw["w_b"], lw["w_out"], lw["ln1_g"], lw["ln1_b"])


def _top16_by_rows(s):
    n = s.shape[0]
    idx = lax.broadcasted_iota(jnp.int32, s.shape, 0).astype(F32)
    k_i = lax.broadcasted_iota(jnp.int32, (PEER_TOPK, s.shape[1]), 0)
    rank = jnp.full(s.shape, NOT_RANKED, F32)
    vals = jnp.zeros((PEER_TOPK, s.shape[1]), F32)
    work = s
    for k in range(PEER_TOPK):
        m = jnp.max(work, axis=0, keepdims=True)
        first = jnp.min(jnp.where(work == m, idx, float(n)), axis=0, keepdims=True)
        sel = idx == first
        rank = jnp.where(sel, float(k), rank)
        work = jnp.where(sel, -jnp.inf, work)
        vals = jnp.where(k_i == k, m, vals)
    return rank, vals


def _candidate_pieces(v1, v2):
    neg = -jnp.inf
    r8 = lax.broadcasted_iota(jnp.int32, (SUBLANES, v1.shape[1]), 0)
    pieces = [v1[0:1, :] + v2[0:8, :], v1[0:1, :] + v2[8:16, :], v1[1:2, :] + v2[0:8, :]]
    for a in range(2, 8):
        nb = PEER_TOPK // (a + 1)
        pieces.append(jnp.where(r8 < nb, v1[a:a + 1, :] + v2[0:8, :], neg))
    pieces.append(v1[8:16, :] + v2[0:1, :])
    return jnp.concatenate(pieces, axis=0)


def _peer_keys_body(x_ref, wq_ref, k1_ref, k2_ref, a_ref, cut_ref, b_ref, r2_ref):
    qp = jnp.dot(x_ref[...].astype(BF16), wq_ref[...], preferred_element_type=F32)
    for h in range(PEER_HEADS):
        q1 = qp[:, (2 * h) * PEER_HALF:(2 * h + 1) * PEER_HALF].astype(BF16)
        q2 = qp[:, (2 * h + 1) * PEER_HALF:(2 * h + 2) * PEER_HALF].astype(BF16)
        nt = (((1,), (1,)), ((), ()))
        s1 = lax.dot_general(k1_ref[h], q1, nt, preferred_element_type=F32)
        s2 = lax.dot_general(k2_ref[h], q2, nt, preferred_element_type=F32)
        r1, v1 = _top16_by_rows(s1)
        r2, v2 = _top16_by_rows(s2)
        cand = _candidate_pieces(v1, v2)
        rc, _ = _top16_by_rows(cand)
        chosen = rc < float(PEER_TOPK)
        top = cand[0:1, :]
        denom = jnp.sum(jnp.where(chosen, jnp.exp(cand - top), 0.0), axis=0, keepdims=True)
        cnt = jnp.where(chosen, 1.0, 0.0)
        n_of_a = [jnp.sum(cnt[0:16, :], axis=0, keepdims=True)]
        for a in range(1, 8):
            n_of_a.append(jnp.sum(cnt[8 + 8 * a:16 + 8 * a, :], axis=0, keepdims=True))
        for a in range(8, 16):
            n_of_a.append(cnt[64 + a:65 + a, :])
        cut = jnp.zeros_like(r1)
        for a in range(PEER_TOPK):
            cut = jnp.where(r1 == float(a), n_of_a[a], cut)
        a_ref[h] = jnp.exp(s1 - v1[0:1, :]) / denom
        cut_ref[h] = cut
        b_ref[h] = jnp.exp(s2 - v2[0:1, :])
        r2_ref[h] = r2


def _peer_keys(x, lw, tm_target=256):
    m = x.shape[0]
    tm = _pick_block(m, tm_target, LANES)
    spec3 = pl.BlockSpec((PEER_HEADS, PEER_N_KEYS, tm), lambda i: (0, 0, i))
    shape3 = jax.ShapeDtypeStruct((PEER_HEADS, PEER_N_KEYS, m), F32)
    return pl.pallas_call(
        _peer_keys_body,
        grid=(m // tm,),
        in_specs=[pl.BlockSpec((tm, D_MODEL), lambda i: (i, 0)),
                  pl.BlockSpec((D_MODEL, 2 * PEER_HEADS * PEER_HALF), lambda i: (0, 0)),
                  pl.BlockSpec((PEER_HEADS, PEER_N_KEYS, PEER_HALF), lambda i: (0, 0, 0)),
                  pl.BlockSpec((PEER_HEADS, PEER_N_KEYS, PEER_HALF), lambda i: (0, 0, 0))],
        out_specs=[spec3, spec3, spec3, spec3],
        out_shape=[shape3, shape3, shape3, shape3],
        compiler_params=_params(("parallel",)),
        name="peer_keys",
    )(x, lw["peer_wq"], lw["peer_k1"], lw["peer_k2"])


def _peer_experts_body(x_ref, u_ref, vt_ref, a_ref, cut_ref, b_ref, r2_ref, o_ref, acc_s, coef_s, *, te):
    j = pl.program_id(1)

    @pl.when(j == 0)
    def _():
        acc_s[...] = jnp.zeros_like(acc_s)

    hid = lax.dot_general(u_ref[...], x_ref[...].astype(BF16), (((1,), (1,)), ((), ())),
                          preferred_element_type=F32)
    act = 0.5 * hid * (1.0 + lax.erf(hid * (1.0 / math.sqrt(2.0))))
    n_i1 = te // PEER_N_KEYS
    for qq in range(n_i1):
        i1 = j * n_i1 + qq
        gate = jnp.zeros((PEER_N_KEYS, x_ref.shape[0]), F32)
        for h in range(PEER_HEADS):
            a = a_ref[h, pl.ds(i1, 1), :]
            cut = cut_ref[h, pl.ds(i1, 1), :]
            gate = gate + jnp.where(r2_ref[h] < cut, b_ref[h], 0.0) * a
        rs = slice(qq * PEER_N_KEYS, (qq + 1) * PEER_N_KEYS)
        coef_s[rs, :] = (gate * act[rs, :]).astype(BF16)
    acc_s[...] = acc_s[...] + jnp.dot(vt_ref[...], coef_s[...], preferred_element_type=F32)

    @pl.when(j == pl.num_programs(1) - 1)
    def _():
        o_ref[...] = acc_s[...].T


def _peer_experts(x, a, cut, b, r2, lw, tm_target=256, te=512):
    m = x.shape[0]
    tm = _pick_block(m, tm_target, LANES)
    n_exp = lw["peer_u"].shape[0]
    spec3 = pl.BlockSpec((PEER_HEADS, PEER_N_KEYS, tm), lambda i, j: (0, 0, i))
    return pl.pallas_call(
        functools.partial(_peer_experts_body, te=te),
        grid=(m // tm, n_exp // te),
        in_specs=[pl.BlockSpec((tm, D_MODEL), lambda i, j: (i, 0)),
                  pl.BlockSpec((te, D_MODEL), lambda i, j: (j, 0)),
                  pl.BlockSpec((D_MODEL, te), lambda i, j: (0, j)),
                  spec3, spec3, spec3, spec3],
        out_specs=pl.BlockSpec((tm, D_MODEL), lambda i, j: (i, 0)),
        out_shape=jax.ShapeDtypeStruct((m, D_MODEL), F32),
        scratch_shapes=[pltpu.VMEM((D_MODEL, tm), F32), pltpu.VMEM((te, tm), BF16)],
        compiler_params=_params(("parallel", "arbitrary")),
        name="peer_experts",
    )(x, lw["peer_u"], lw["peer_vt"], a, cut, b, r2)


def _post_body(x_ref, f_ref, p_ref, g_ref, b_ref, wg_ref, wp_ref, o_ref, *, alpha):
    x2 = _layer_norm(alpha * x_ref[...] + f_ref[...], g_ref[...], b_ref[...])
    gate = jax.nn.sigmoid(jnp.dot(x2.astype(BF16), wg_ref[...], preferred_element_type=F32))
    emb = jnp.dot(p_ref[...].astype(BF16), wp_ref[...], preferred_element_type=F32)
    o_ref[...] = x2 + gate * emb


def _post(x, ffn, p_emb, lw, alpha, tm_target=512):
    m = x.shape[0]
    tm = _pick_block(m, tm_target)
    ple = p_emb.shape[1]
    tok = lambda i: (i, 0)
    const = lambda i: (0, 0)
    return pl.pallas_call(
        functools.partial(_post_body, alpha=alpha),
        grid=(m // tm,),
        in_specs=[pl.BlockSpec((tm, D_MODEL), tok), pl.BlockSpec((tm, D_MODEL), tok),
                  pl.BlockSpec((tm, ple), tok),
                  pl.BlockSpec((1, D_MODEL), const), pl.BlockSpec((1, D_MODEL), const),
                  pl.BlockSpec((D_MODEL, D_MODEL), const), pl.BlockSpec((ple, D_MODEL), const)],
        out_specs=pl.BlockSpec((tm, D_MODEL), tok),
        out_shape=jax.ShapeDtypeStruct((m, D_MODEL), F32),
        compiler_params=_params(("parallel",)),
        name="ln2_ple",
    )(x, ffn, p_emb, lw["ln2_g"], lw["ln2_b"], lw["w_ple_gate"], lw["w_ple_proj"])


def _layer_weights(i, w_in, conv_w, conv_b, dt_bias, a_log, d_skip, ssm_norm_w, fox_f_bias, w_branch_a,
                   w_branch_b, w_out, ln1_g, ln1_b, ln2_g, ln2_b, peer_w_q, peer_sub_keys, peer_u, peer_v,
                   w_ple_gate, w_ple_proj):
    sizes = (SSM_D_INNER, SSM_CONV_DIM, SSM_HEADS, FOX_WIDTH, FOX_WIDTH, FOX_WIDTH, FOX_HEADS, D_MODEL, D_MODEL)
    offs = np.concatenate([[0], np.cumsum(sizes)])
    w = w_in[i]
    cols = {n: w[:, offs[j]:offs[j + 1]] for j, n in enumerate(("z", "xbc", "dt", "q", "k", "v", "f", "ga", "gb"))}
    small = jnp.concatenate([cols["dt"], cols["f"],
                             jnp.zeros((D_MODEL, LANES - SSM_HEADS - FOX_HEADS), F32)], axis=1)
    zpad = jnp.zeros((LANES - SSM_HEADS - FOX_HEADS,), F32)
    lw = {n: cols[n].astype(BF16) for n in ("z", "xbc", "q", "k", "v", "ga", "gb")}
    lw["small"] = small.astype(BF16)
    lw["conv_w8"] = jnp.concatenate([conv_w[i], jnp.zeros((SUBLANES - SSM_CONV_W, SSM_CONV_DIM), F32)], axis=0)
    lw["conv_b"] = conv_b[i][None, :]
    lw["bias128"] = jnp.concatenate([dt_bias[i], fox_f_bias[i], zpad])[None, :]
    lw["aneg128"] = jnp.concatenate([-jnp.exp(a_log[i]), jnp.zeros((LANES - SSM_HEADS,), F32)])[None, :]
    lw["dskip_e"] = jnp.repeat(d_skip[i], SSM_HEAD_DIM)[None, :]
    lw["norm_w"] = ssm_norm_w[i][None, :]
    head_of_col = np.arange(SSM_D_INNER) // SSM_HEAD_DIM
    lw["expand"] = jnp.asarray(np.arange(LANES)[:, None] == head_of_col[None, :], F32)
    lw["tri"] = jnp.asarray(np.tril(np.ones((SSM_CHUNK, SSM_CHUNK), np.float32)))
    lw["w_a"] = w_branch_a[i].astype(BF16)
    lw["w_b"] = w_branch_b[i].astype(BF16)
    lw["w_out"] = w_out[i].astype(BF16)
    lw["ln1_g"], lw["ln1_b"] = ln1_g[i][None, :], ln1_b[i][None, :]
    lw["ln2_g"], lw["ln2_b"] = ln2_g[i][None, :], ln2_b[i][None, :]
    lw["peer_wq"] = peer_w_q[i].astype(BF16)
    lw["peer_k1"] = peer_sub_keys[i, 0].astype(BF16)
    lw["peer_k2"] = peer_sub_keys[i, 1].astype(BF16)
    lw["peer_u"] = peer_u[i].astype(BF16)
    lw["peer_vt"] = peer_v[i].T.astype(BF16)
    lw["w_ple_gate"] = w_ple_gate[i].astype(BF16)
    lw["w_ple_proj"] = w_ple_proj[i].astype(BF16)
    return lw


def _trunk_layer(x, p_emb, lw, conv_state, ssm_state, cache_k, cache_v, cache_lf_t, page_table, *, layer,
                 bp, lp, bs, ls, alpha):
    tp = bp * lp
    z = _matmul(x, lw["z"])
    xbc = _matmul(x, lw["xbc"])
    small = _matmul(x, lw["small"])
    q = _matmul(x, lw["q"], out_dtype=BF16, scale=FOX_SCALE)
    k = _matmul(x, lw["k"])
    v = _matmul(x, lw["v"])
    ga = _matmul(x, lw["ga"])
    gb = _matmul(x, lw["gb"])

    zero_conv = jnp.zeros((bp, SUBLANES, SSM_CONV_DIM), F32)
    zero_h = jnp.zeros((bp, SSM_D_INNER, SSM_D_STATE), F32)
    ya_p, lf_p, c_p, h_p = _ssd(xbc, z, small, zero_conv, zero_h, lw, row0=0, batch=bp, seqlen=lp)
    conv8 = jnp.concatenate([jnp.zeros((bs, SUBLANES - (SSM_CONV_W - 1), SSM_CONV_DIM), F32), conv_state], axis=1)
    ya_s, lf_s, c_s, h_s = _ssd(xbc, z, small, conv8, ssm_state.reshape(bs, SSM_D_INNER, SSM_D_STATE), lw,
                                row0=tp, batch=bs, seqlen=ls)
    logf_p = lf_p[:, F_LO:F_HI]
    logf_s = lf_s[:, F_LO:F_HI]
    cum_p = c_p[:, F_LO:F_HI]
    cum_s = c_s[:, F_LO:F_HI].reshape(bs, ls, FOX_HEADS)

    yb_p = _fox_prompt(q, k, v, cum_p, cum_p.T, batch=bp, seqlen=lp)
    page = cache_k.shape[2]
    nrow = LANES
    qs = q[tp:].reshape(bs, ls, FOX_HEADS, FOX_HEAD_DIM)
    eye = jnp.eye(FOX_HEADS, dtype=BF16)
    qbd = (qs.transpose(0, 2, 1, 3)[:, :, :, None, :] * eye[None, :, None, :, None]).reshape(
        bs, FOX_HEADS * ls, FOX_WIDTH)
    qbd = jnp.pad(qbd, ((0, 0), (0, nrow - FOX_HEADS * ls), (0, 0)))
    ccol = jnp.pad(cum_s.transpose(0, 2, 1).reshape(bs, FOX_HEADS * ls, 1),
                   ((0, 0), (0, nrow - FOX_HEADS * ls), (0, 0)))
    knew = jnp.pad(k[tp:].reshape(bs, ls, FOX_WIDTH), ((0, 0), (0, page - ls), (0, 0)))
    vnew = jnp.pad(v[tp:].reshape(bs, ls, FOX_WIDTH), ((0, 0), (0, page - ls), (0, 0)))
    cnew_t = jnp.pad(cum_s.transpose(0, 2, 1), ((0, 0), (0, 0), (0, page - ls)))
    yb_s = _fox_sample(qbd, ccol, knew, vnew, cnew_t, cache_k, cache_v, cache_lf_t, page_table,
                       layer=layer, n_new=ls).reshape(bs * ls, FOX_WIDTH)

    ya = jnp.concatenate([ya_p, ya_s], axis=0)
    yb = jnp.concatenate([yb_p, yb_s], axis=0)
    x1 = _merge(x, ya, yb, ga, gb, lw, alpha)
    a, cut, b, r2 = _peer_keys(x1, lw)
    ffn = _peer_experts(x1, a, cut, b, r2, lw)
    x2 = _post(x1, ffn, p_emb, lw, alpha)

    hshape = (FOX_HEADS, FOX_HEAD_DIM)
    sshape = (SSM_HEADS, SSM_HEAD_DIM, SSM_D_STATE)
    xbc_p = xbc[:tp].reshape(bp, lp, SSM_CONV_DIM)
    xbc_s = xbc[tp:].reshape(bs, ls, SSM_CONV_DIM)
    prompt_out = (k[:tp].reshape(bp, lp, *hshape), v[:tp].reshape(bp, lp, *hshape),
                  logf_p.reshape(bp, lp, FOX_HEADS), h_p.reshape(bp, *sshape),
                  xbc_p[:, lp - (SSM_CONV_W - 1):])
    sample_out = (k[tp:].reshape(bs, ls, *hshape), v[tp:].reshape(bs, ls, *hshape),
                  logf_s.reshape(bs, ls, FOX_HEADS), h_s.reshape(bs, *sshape),
                  jnp.concatenate([conv_state, xbc_s], axis=1)[:, ls:])
    return x2, prompt_out, sample_out


def kernel(x_prompt, x_sample, cache_k, cache_v, cache_logf, state_ssm, state_conv, page_table, p_prompt, p_sample, w_in, conv_w, conv_b, dt_bias, a_log, d_skip, ssm_norm_w, fox_f_bias, w_branch_a, w_branch_b, w_out, ln1_g, ln1_b, ln2_g, ln2_b, peer_w_q, peer_sub_keys, peer_u, peer_v, w_ple_gate, w_ple_proj):
    depth = w_in.shape[0]
    bp, lp, _ = x_prompt.shape
    bs, ls, _ = x_sample.shape
    alpha = (2.0 * depth) ** 0.25
    n_pool, page = cache_k.shape[1], cache_k.shape[2]
    ck = cache_k.reshape(depth, n_pool, page, FOX_WIDTH)
    cv = cache_v.reshape(depth, n_pool, page, FOX_WIDTH)
    clf_t = cache_logf.transpose(0, 1, 3, 2)
    x = jnp.concatenate([x_prompt.reshape(bp * lp, D_MODEL), x_sample.reshape(bs * ls, D_MODEL)], axis=0)
    outs_p, outs_s = [], []
    for i in range(depth):
        lw = _layer_weights(i, w_in, conv_w, conv_b, dt_bias, a_log, d_skip, ssm_norm_w, fox_f_bias, w_branch_a,
                            w_branch_b, w_out, ln1_g, ln1_b, ln2_g, ln2_b, peer_w_q, peer_sub_keys, peer_u,
                            peer_v, w_ple_gate, w_ple_proj)
        p_emb = jnp.concatenate([p_prompt[i].reshape(bp * lp, -1), p_sample[i].reshape(bs * ls, -1)], axis=0)
        x, po, so = _trunk_layer(x, p_emb, lw, state_conv[i], state_ssm[i], ck, cv, clf_t, page_table,
                                 layer=i, bp=bp, lp=lp, bs=bs, ls=ls, alpha=alpha)
        outs_p.append(po)
        outs_s.append(so)
    stack = lambda outs, j: jnp.stack([o[j] for o in outs])
    tp = bp * lp
    return (x[:tp].reshape(bp, lp, D_MODEL), x[tp:].reshape(bs, ls, D_MODEL),
            stack(outs_p, 0), stack(outs_p, 1), stack(outs_p, 2), stack(outs_p, 3), stack(outs_p, 4),
            stack(outs_s, 0), stack(outs_s, 1), stack(outs_s, 2), stack(outs_s, 3), stack(outs_s, 4))
```

```python
import functools
import math

import numpy as np
import jax
import jax.numpy as jnp
from jax import lax
from jax.experimental import pallas as pl
from jax.experimental.pallas import tpu as pltpu

F32 = jnp.float32
BF16 = jnp.bfloat16
HIGHEST = lax.Precision.HIGHEST

D_MODEL = 1024
SSM_D_INNER = 2048
SSM_HEAD_DIM = 64
SSM_HEADS = 32
SSM_GROUPS = 4
SSM_D_STATE = 128
SSM_CONV_W = 4
SSM_CONV_DIM = SSM_D_INNER + 2 * SSM_GROUPS * SSM_D_STATE
SSM_CHUNK = 128
FOX_HEAD_DIM = 128
FOX_HEADS = 8
FOX_WIDTH = FOX_HEADS * FOX_HEAD_DIM
FOX_SCALE = FOX_HEAD_DIM ** -0.5
PEER_HEADS = 8
PEER_N_KEYS = 128
PEER_TOPK = 16
PEER_HALF = 128
LN_EPS = 1e-5
RMS_EPS = 1e-6
LOG2E = math.log2(math.e)

LANES = 128
SUBLANES = 8
VMEM_LIMIT_BYTES = 56 * 1024 * 1024
TOKEN_TILE = 512

DT_LO, DT_HI = 0, SSM_HEADS
F_LO, F_HI = SSM_HEADS, SSM_HEADS + FOX_HEADS
NOT_RANKED = 99.0


def _pick_block(n, target, mult=SUBLANES):
    for d in range(min(n, target), 0, -1):
        if n % d == 0 and d % mult == 0:
            return d
    return n


def _params(sem):
    return pltpu.CompilerParams(dimension_semantics=sem, vmem_limit_bytes=VMEM_LIMIT_BYTES)


def _mm_body(x_ref, w_ref, *o_refs):
    acc = jnp.dot(x_ref[...].astype(BF16), w_ref[...], preferred_element_type=F32)
    for o_ref in o_refs:
        o_ref[...] = acc.astype(o_ref.dtype)


def _matmul(x, w, out_dtypes=(F32,), tm_target=1280, tn_target=512):
    m, k = x.shape
    n = w.shape[1]
    tm = _pick_block(m, tm_target)
    tn = _pick_block(n, tn_target, LANES)
    outs = pl.pallas_call(
        _mm_body,
        grid=(m // tm, n // tn),
        in_specs=[pl.BlockSpec((tm, k), lambda i, j: (i, 0)),
                  pl.BlockSpec((k, tn), lambda i, j: (0, j))],
        out_specs=[pl.BlockSpec((tm, tn), lambda i, j: (i, j)) for _ in out_dtypes],
        out_shape=[jax.ShapeDtypeStruct((m, n), dt) for dt in out_dtypes],
        compiler_params=_params(("parallel", "parallel")),
        name="proj_matmul",
    )(x, w)
    return outs[0] if len(out_dtypes) == 1 else outs


def _mm_t_body(w_ref, x_ref, o_ref, *, scale):
    acc = lax.dot_general(w_ref[...], x_ref[...].astype(BF16), (((1,), (1,)), ((), ())),
                          preferred_element_type=F32)
    if scale is not None:
        acc = acc * scale
    o_ref[...] = acc.astype(o_ref.dtype)


def _matmul_t(w_t, x, out_dtype=BF16, scale=None, tm_target=1024, tn_target=512):
    n, k = w_t.shape
    m = x.shape[0]
    tm = _pick_block(m, tm_target, LANES)
    tn = _pick_block(n, tn_target)
    return pl.pallas_call(
        functools.partial(_mm_t_body, scale=scale),
        grid=(m // tm, n // tn),
        in_specs=[pl.BlockSpec((tn, k), lambda i, j: (j, 0)),
                  pl.BlockSpec((tm, k), lambda i, j: (i, 0))],
        out_specs=pl.BlockSpec((tn, tm), lambda i, j: (j, i)),
        out_shape=jax.ShapeDtypeStruct((n, m), out_dtype),
        compiler_params=_params(("parallel", "parallel")),
        name="proj_matmul_t",
    )(w_t, x)


def _ssd_body(xbc_ref, z_ref, sm_ref, cst_ref, h0_ref, cw_ref, cb_ref, bias_ref, aneg_ref,
              dsk_ref, nw_ref, e_ref, tri_ref,
              y_ref, lf_ref, c_ref, hfin_ref,
              xp_s, ht_s, cc_s, y_s, *, rows, nc):
    q = SSM_CHUNK
    ci = pl.program_id(1)

    @pl.when(ci == 0)
    def _():
        xp_s[0:SUBLANES, :] = cst_ref[0]
        ht_s[...] = h0_ref[0].T
        cc_s[...] = jnp.zeros_like(cc_s)

    if rows < q:
        pad = q - rows
        xp_s[SUBLANES:SUBLANES + rows, :] = xbc_ref[...]
        xp_s[SUBLANES + rows:SUBLANES + q, :] = jnp.zeros((pad, SSM_CONV_DIM), F32)
        z = jnp.concatenate([z_ref[...], jnp.zeros((pad, SSM_D_INNER), F32)], axis=0)
        sm = jnp.concatenate([sm_ref[...], jnp.zeros((pad, LANES), F32)], axis=0)
    else:
        xp_s[SUBLANES:SUBLANES + q, :] = xbc_ref[...]
        z = z_ref[...]
        sm = sm_ref[...]

    w = cw_ref[...]
    base = SUBLANES - (SSM_CONV_W - 1)
    conv = cb_ref[...]
    for t in range(SSM_CONV_W):
        conv = conv + xp_s[base + t:base + t + q, :] * w[t:t + 1, :]
    act = jax.nn.silu(conv)
    xp_s[0:SUBLANES, :] = xp_s[q:q + SUBLANES, :]

    lane = lax.broadcasted_iota(jnp.int32, (q, LANES), 1)
    row = lax.broadcasted_iota(jnp.int32, (q, LANES), 0)
    valid = row < rows
    sm = sm + bias_ref[...]
    dt = jnp.where(valid, jax.nn.softplus(sm), 0.0)
    logf = jnp.where(valid, jax.nn.log_sigmoid(sm), 0.0)
    is_dt = lane < DT_HI
    is_f = (lane >= F_LO) & (lane < F_HI)
    steps = jnp.where(is_dt, dt * aneg_ref[...], jnp.where(is_f, logf, 0.0))
    cum = jnp.dot(tri_ref[...], steps, precision=HIGHEST, preferred_element_type=F32)
    cfull = cum + cc_s[...]
    cc_s[...] = jnp.where(is_f[0:1, :], cfull[q - 1:q, :], 0.0)
    lf_ref[...] = logf[0:rows, :]
    c_ref[...] = cfull[0:rows, :]

    acum = cum
    ea = jnp.exp(acum)
    de = jnp.exp(acum[q - 1:q, :] - acum)
    e = e_ref[...]
    dt_e = jnp.dot(dt, e, precision=HIGHEST, preferred_element_type=F32)
    ea_e = jnp.dot(ea, e, precision=HIGHEST, preferred_element_type=F32)
    de_e = jnp.dot(de, e, precision=HIGHEST, preferred_element_type=F32)
    cd_e = ea_e[q - 1:q, :]

    xs = act[:, 0:SSM_D_INNER]
    xd = xs * dt_e
    acum_t = acum.T
    li = lax.broadcasted_iota(jnp.int32, (q, q), 0)
    si = lax.broadcasted_iota(jnp.int32, (q, q), 1)
    causal = li >= si
    half_lane = lax.broadcasted_iota(jnp.int32, (q, LANES), 1) < SSM_HEAD_DIM
    gw = SSM_D_INNER // SSM_GROUPS
    hpg = SSM_HEADS // SSM_GROUPS
    for g in range(SSM_GROUPS):
        bg = act[:, SSM_D_INNER + g * SSM_D_STATE:SSM_D_INNER + (g + 1) * SSM_D_STATE]
        cg = act[:, SSM_D_INNER + (SSM_GROUPS + g) * SSM_D_STATE:SSM_D_INNER + (SSM_GROUPS + g + 1) * SSM_D_STATE]
        bg16 = bg.astype(BF16)
        cg16 = cg.astype(BF16)
        cb = lax.dot_general(cg16, bg16, (((1,), (1,)), ((), ())), preferred_element_type=F32)
        for pr in range(hpg // 2):
            h0 = g * hpg + 2 * pr
            ms = []
            for h in (h0, h0 + 1):
                seg = acum[:, h:h + 1] - acum_t[h:h + 1, :]
                dec = jnp.exp(jnp.where(causal, seg, -jnp.inf))
                ms.append((cb * dec).astype(BF16))
            lhs = jnp.concatenate(ms, axis=1)
            xp = xd[:, h0 * SSM_HEAD_DIM:(h0 + 2) * SSM_HEAD_DIM]
            rhs = jnp.concatenate([jnp.where(half_lane, xp, 0.0), jnp.where(half_lane, 0.0, xp)],
                                  axis=0).astype(BF16)
            y_s[:, h0 * SSM_HEAD_DIM:(h0 + 2) * SSM_HEAD_DIM] = jnp.dot(
                lhs, rhs, preferred_element_type=F32)
        sl = slice(g * gw, (g + 1) * gw)
        ht_g = ht_s[:, sl]
        y_off = jnp.dot(cg16, ht_g.astype(BF16), preferred_element_type=F32) * ea_e[:, sl]
        y_s[:, sl] = y_s[:, sl] + y_off
        st = jnp.dot(bg.T.astype(BF16), (xd[:, sl] * de_e[:, sl]).astype(BF16), preferred_element_type=F32)
        ht_s[:, sl] = ht_g * cd_e[:, sl] + st

    y = y_s[...] + dsk_ref[...] * xs
    y = y * jax.nn.silu(z)
    nw = nw_ref[...]
    for g in range(SSM_GROUPS):
        sl = slice(g * gw, (g + 1) * gw)
        yg = y[:, sl]
        ms = jnp.sum(yg * yg, axis=1, keepdims=True) * (1.0 / gw)
        y_ref[:, sl] = (yg * lax.rsqrt(ms + RMS_EPS) * nw[:, sl])[0:rows, :].astype(y_ref.dtype)

    @pl.when(ci == nc - 1)
    def _():
        hfin_ref[0] = ht_s[...].T


def _ssd(xbc, z, small, conv_state8, h0, lw, *, row0, batch, seqlen):
    q = SSM_CHUNK
    rows = q if seqlen % q == 0 else seqlen
    assert rows % SUBLANES == 0 and rows <= q and row0 % rows == 0
    nc = seqlen // rows
    blk0 = row0 // rows
    tok = lambda b, c: (blk0 + b * nc + c, 0)
    out_tok = lambda b, c: (b * nc + c, 0)
    const2 = lambda b, c: (0, 0)
    n = batch * seqlen
    return pl.pallas_call(
        functools.partial(_ssd_body, rows=rows, nc=nc),
        grid=(batch, nc),
        in_specs=[pl.BlockSpec((rows, SSM_CONV_DIM), tok),
                  pl.BlockSpec((rows, SSM_D_INNER), tok),
                  pl.BlockSpec((rows, LANES), tok),
                  pl.BlockSpec((1, SUBLANES, SSM_CONV_DIM), lambda b, c: (b, 0, 0)),
                  pl.BlockSpec((1, SSM_D_INNER, SSM_D_STATE), lambda b, c: (b, 0, 0)),
                  pl.BlockSpec((SUBLANES, SSM_CONV_DIM), const2),
                  pl.BlockSpec((1, SSM_CONV_DIM), const2),
                  pl.BlockSpec((1, LANES), const2),
                  pl.BlockSpec((1, LANES), const2),
                  pl.BlockSpec((1, SSM_D_INNER), const2),
                  pl.BlockSpec((1, SSM_D_INNER), const2),
                  pl.BlockSpec((LANES, SSM_D_INNER), const2),
                  pl.BlockSpec((q, q), const2)],
        out_specs=[pl.BlockSpec((rows, SSM_D_INNER), out_tok),
                   pl.BlockSpec((rows, LANES), out_tok),
                   pl.BlockSpec((rows, LANES), out_tok),
                   pl.BlockSpec((1, SSM_D_INNER, SSM_D_STATE), lambda b, c: (b, 0, 0))],
        out_shape=[jax.ShapeDtypeStruct((n, SSM_D_INNER), BF16),
                   jax.ShapeDtypeStruct((n, LANES), F32),
                   jax.ShapeDtypeStruct((n, LANES), F32),
                   jax.ShapeDtypeStruct((batch, SSM_D_INNER, SSM_D_STATE), F32)],
        scratch_shapes=[pltpu.VMEM((q + SUBLANES, SSM_CONV_DIM), F32),
                        pltpu.VMEM((SSM_D_STATE, SSM_D_INNER), F32),
                        pltpu.VMEM((1, LANES), F32),
                        pltpu.VMEM((q, SSM_D_INNER), F32)],
        compiler_params=_params(("parallel", "arbitrary")),
        name="ssd_scan",
    )(xbc, z, small, conv_state8, h0, lw["conv_w8"], lw["conv_b"], lw["bias128"], lw["aneg128"],
      lw["dskip_e"], lw["norm_w"], lw["expand"], lw["tri"])


FLASH_Q_STRIP = 256
FLASH_K_TILE = 128


def _flash_body(qt_ref, k_ref, vt_ref, c_ref, o_ref, m_s, l_s, acc_s, *, tq):
    qi = pl.program_id(1)
    ki = pl.program_id(2)
    n_strip = tq // FLASH_Q_STRIP
    n_kt = tq // FLASH_K_TILE

    @pl.when(ki == 0)
    def _():
        m_s[...] = jnp.full_like(m_s, -jnp.inf)
        l_s[...] = jnp.zeros_like(l_s)
        acc_s[...] = jnp.zeros_like(acc_s)

    def block(diagonal):
        for h in range(FOX_HEADS):
            hs = slice(h * FOX_HEAD_DIM, (h + 1) * FOX_HEAD_DIM)
            ck = c_ref[:, h:h + 1] * LOG2E
            for st in range(n_strip):
                cs = slice(st * FLASH_Q_STRIP, (st + 1) * FLASH_Q_STRIP)
                q_t = qt_ref[hs, cs]
                m_prev = m_s[h:h + 1, cs]
                l_prev = l_s[h:h + 1, cs]
                for kt in range(n_kt):
                    k_lo, k_hi = kt * FLASH_K_TILE, (kt + 1) * FLASH_K_TILE
                    q_lo, q_hi = st * FLASH_Q_STRIP, (st + 1) * FLASH_Q_STRIP
                    if diagonal and k_lo > q_hi - 1:
                        continue
                    ks = slice(k_lo, k_hi)
                    s = jnp.dot(k_ref[ks, hs], q_t, preferred_element_type=F32)
                    s = s - ck[ks, :]
                    if diagonal and k_hi - 1 > q_lo:
                        key_i = k_lo + lax.broadcasted_iota(jnp.int32, s.shape, 0)
                        qry_i = q_lo + lax.broadcasted_iota(jnp.int32, s.shape, 1)
                        s = jnp.where(key_i <= qry_i, s, -jnp.inf)
                    m_new = jnp.maximum(m_prev, jnp.max(s, axis=0, keepdims=True))
                    alpha = jnp.exp2(m_prev - m_new)
                    p = jnp.exp2(s - m_new)
                    l_prev = alpha * l_prev + jnp.sum(p, axis=0, keepdims=True)
                    acc_s[hs, cs] = acc_s[hs, cs] * alpha + jnp.dot(vt_ref[hs, ks], p.astype(BF16),
                                                                    preferred_element_type=F32)
                    m_prev = m_new
                m_s[h:h + 1, cs] = m_prev
                l_s[h:h + 1, cs] = l_prev

    @pl.when(ki < qi)
    def _():
        block(False)

    @pl.when(ki == qi)
    def _():
        block(True)
        for h in range(FOX_HEADS):
            hs = slice(h * FOX_HEAD_DIM, (h + 1) * FOX_HEAD_DIM)
            o_ref[:, hs] = (acc_s[hs, :] / l_s[h:h + 1, :]).T.astype(o_ref.dtype)


def _fox_prompt(q_t, k16, v_t, c, *, batch, seqlen, tq_target=512):
    tq = _pick_block(seqlen, tq_target, FLASH_Q_STRIP)
    assert tq % FLASH_Q_STRIP == 0 and tq % FLASH_K_TILE == 0
    nq = seqlen // tq
    kv_blk = lambda b, i, j: b * nq + jnp.minimum(i, j)
    return pl.pallas_call(
        functools.partial(_flash_body, tq=tq),
        grid=(batch, nq, nq),
        in_specs=[pl.BlockSpec((FOX_WIDTH, tq), lambda b, i, j: (0, b * nq + i)),
                  pl.BlockSpec((tq, FOX_WIDTH), lambda b, i, j: (kv_blk(b, i, j), 0)),
                  pl.BlockSpec((FOX_WIDTH, tq), lambda b, i, j: (0, kv_blk(b, i, j))),
                  pl.BlockSpec((tq, FOX_HEADS), lambda b, i, j: (kv_blk(b, i, j), 0))],
        out_specs=pl.BlockSpec((tq, FOX_WIDTH), lambda b, i, j: (b * nq + i, 0)),
        out_shape=jax.ShapeDtypeStruct((batch * seqlen, FOX_WIDTH), BF16),
        scratch_shapes=[pltpu.VMEM((FOX_HEADS, tq), F32), pltpu.VMEM((FOX_HEADS, tq), F32),
                        pltpu.VMEM((FOX_WIDTH, tq), F32)],
        compiler_params=_params(("parallel", "parallel", "arbitrary")),
        name="fox_prompt",
    )(q_t, k16, v_t, c)


PAGES_PER_STEP = 2


def _paged_body(pt_ref, qbd_ref, knew_ref, vnew_ref, cnew_t_ref, *rest, n_new, n_pages, pps):
    del pt_ref
    kp_refs, vp_refs, lf_refs = rest[0:pps], rest[pps:2 * pps], rest[2 * pps:3 * pps]
    o_ref, m_s, l_s, acc_s, carry_s = rest[3 * pps:]
    step = pl.program_id(1)
    page = kp_refs[0].shape[2] // FOX_HEADS
    nrow = FOX_HEADS * n_new

    def rows_of_heads(x8):
        return jnp.concatenate([jnp.broadcast_to(x8[h:h + 1, :], (n_new, x8.shape[1]))
                                for h in range(FOX_HEADS)], axis=0)

    def update(k16, v16, key_bias, mask):
        s = lax.dot_general(qbd_ref[0], k16, (((1,), (1,)), ((), ())), preferred_element_type=F32)
        s = s + rows_of_heads(key_bias * LOG2E)
        if mask is not None:
            s = jnp.where(mask, s, -jnp.inf)
        m_prev = m_s[...]
        m_new = jnp.maximum(m_prev, jnp.max(s, axis=1, keepdims=True))
        alpha = jnp.exp2(m_prev - m_new)
        p = jnp.exp2(s - m_new)
        l_s[...] = alpha * l_s[...] + jnp.sum(p, axis=1, keepdims=True)
        m_s[...] = m_new
        acc_s[...] = acc_s[...] * alpha + jnp.dot(p.astype(BF16), v16, preferred_element_type=F32)

    def head_major(ref):
        return jnp.concatenate([ref[0, 0, pl.ds(h, page, stride=FOX_HEADS), :].astype(BF16)
                                for h in range(FOX_HEADS)], axis=1)

    @pl.when(step == 0)
    def _():
        m_s[...] = jnp.full_like(m_s, -jnp.inf)
        l_s[...] = jnp.zeros_like(l_s)
        acc_s[...] = jnp.zeros_like(acc_s)
        carry_s[...] = jnp.zeros_like(carry_s)
        rr = lax.broadcasted_iota(jnp.int32, (nrow, page), 0)
        kk = lax.broadcasted_iota(jnp.int32, (nrow, page), 1)
        update(knew_ref[0].astype(BF16), vnew_ref[0].astype(BF16), -cnew_t_ref[0], kk <= rr % n_new)

    @pl.when(step > 0)
    def _():
        ii = lax.broadcasted_iota(jnp.int32, (page, page), 0)
        jj = lax.broadcasted_iota(jnp.int32, (page, page), 1)
        later = jnp.where(ii > jj, 1.0, 0.0)
        for r in range(pps):
            lf_t = lf_refs[r][0, 0]
            suffix = jnp.dot(lf_t, later, precision=HIGHEST, preferred_element_type=F32) + carry_s[...]
            carry_s[...] = carry_s[...] + jnp.sum(lf_t, axis=1, keepdims=True)
            update(head_major(kp_refs[r]), head_major(vp_refs[r]), suffix, None)

    @pl.when(step == pl.num_programs(1) - 1)
    def _():
        for h in range(FOX_HEADS):
            rs = slice(h * n_new, (h + 1) * n_new)
            cs = slice(h * FOX_HEAD_DIM, (h + 1) * FOX_HEAD_DIM)
            o_ref[0, :, cs] = (acc_s[rs, cs] / l_s[rs, :]).astype(o_ref.dtype)


def _fox_sample(qbd, knew, vnew, cnew_t, cache_k, cache_v, cache_lf_t, page_table, *, layer, n_new):
    bsz, n_pages = page_table.shape
    page = cache_k.shape[2] // FOX_HEADS
    nrow = qbd.shape[1]
    pps = PAGES_PER_STEP if n_pages % PAGES_PER_STEP == 0 else 1
    n_steps = n_pages // pps

    def page_of(r):
        return lambda b, s, pt: pt[b, n_pages - 1 - ((jnp.maximum(s, 1) - 1) * pps + r)]

    kv_maps = [(lambda b, s, pt, f=page_of(r): (layer, f(b, s, pt), 0, 0)) for r in range(pps)]
    lf_maps = kv_maps
    per_b = lambda b, s, pt: (b, 0, 0)
    kv_block = (1, 1, page * FOX_HEADS, FOX_HEAD_DIM)
    grid_spec = pltpu.PrefetchScalarGridSpec(
        num_scalar_prefetch=1,
        grid=(bsz, n_steps + 1),
        in_specs=[pl.BlockSpec((1, nrow, FOX_WIDTH), per_b),
                  pl.BlockSpec((1, page, FOX_WIDTH), per_b),
                  pl.BlockSpec((1, page, FOX_WIDTH), per_b),
                  pl.BlockSpec((1, FOX_HEADS, page), per_b)]
                 + [pl.BlockSpec(kv_block, m) for m in kv_maps]
                 + [pl.BlockSpec(kv_block, m) for m in kv_maps]
                 + [pl.BlockSpec((1, 1, FOX_HEADS, page), m) for m in lf_maps],
        out_specs=pl.BlockSpec((1, n_new, FOX_WIDTH), per_b),
        scratch_shapes=[pltpu.VMEM((nrow, 1), F32), pltpu.VMEM((nrow, 1), F32),
                        pltpu.VMEM((nrow, FOX_WIDTH), F32), pltpu.VMEM((FOX_HEADS, 1), F32)])
    return pl.pallas_call(
        functools.partial(_paged_body, n_new=n_new, n_pages=n_pages, pps=pps),
        grid_spec=grid_spec,
        out_shape=jax.ShapeDtypeStruct((bsz, n_new, FOX_WIDTH), BF16),
        compiler_params=_params(("parallel", "arbitrary")),
        name="fox_sample",
    )(page_table, qbd, knew, vnew, cnew_t, *([cache_k] * pps), *([cache_v] * pps), *([cache_lf_t] * pps))


def _layer_norm(r, g, b):
    mu = jnp.mean(r, axis=1, keepdims=True)
    d = r - mu
    var = jnp.mean(d * d, axis=1, keepdims=True)
    return d * lax.rsqrt(var + LN_EPS) * g + b


def _merge_body(x_ref, ya_ref, yb_ref, ga_ref, gb_ref, wa_ref, wb_ref, wo_ref, g_ref, b_ref, o_ref, o16_ref,
                *, alpha):
    a = jnp.dot(ya_ref[...], wa_ref[...], preferred_element_type=F32)
    b = jnp.dot(yb_ref[...], wb_ref[...], preferred_element_type=F32)
    merged = jax.nn.sigmoid(ga_ref[...]) * a + jax.nn.sigmoid(gb_ref[...]) * b
    r = alpha * x_ref[...] + jnp.dot(merged.astype(BF16), wo_ref[...], preferred_element_type=F32)
    y = _layer_norm(r, g_ref[...], b_ref[...])
    o_ref[...] = y
    o16_ref[...] = y.astype(BF16)


def _merge(x, ya, yb, ga, gb, lw, alpha, tm_target=256):
    m = x.shape[0]
    tm = _pick_block(m, tm_target)
    tok = lambda i: (i, 0)
    const = lambda i: (0, 0)
    return pl.pallas_call(
        functools.partial(_merge_body, alpha=alpha),
        grid=(m // tm,),
        in_specs=[pl.BlockSpec((tm, D_MODEL), tok), pl.BlockSpec((tm, SSM_D_INNER), tok),
                  pl.BlockSpec((tm, FOX_WIDTH), tok), pl.BlockSpec((tm, D_MODEL), tok),
                  pl.BlockSpec((tm, D_MODEL), tok),
                  pl.BlockSpec((SSM_D_INNER, D_MODEL), const), pl.BlockSpec((FOX_WIDTH, D_MODEL), const),
                  pl.BlockSpec((D_MODEL, D_MODEL), const),
                  pl.BlockSpec((1, D_MODEL), const), pl.BlockSpec((1, D_MODEL), const)],
        out_specs=[pl.BlockSpec((tm, D_MODEL), tok), pl.BlockSpec((tm, D_MODEL), tok)],
        out_shape=[jax.ShapeDtypeStruct((m, D_MODEL), F32), jax.ShapeDtypeStruct((m, D_MODEL), BF16)],
        compiler_params=_params(("parallel",)),
        name="merge_ln1",
    )(x, ya, yb, ga, gb, lw["w_a"], lw["w_b"], lw["w_out"], lw["ln1_g"], lw["ln1_b"])


def _top16_by_rows(s):
    n = s.shape[0]
    idx = lax.broadcasted_iota(jnp.int32, s.shape, 0).astype(F32)
    k_i = lax.broadcasted_iota(jnp.int32, (PEER_TOPK, s.shape[1]), 0)
    rank = jnp.full(s.shape, NOT_RANKED, F32)
    vals = jnp.zeros((PEER_TOPK, s.shape[1]), F32)
    work = s
    for k in range(PEER_TOPK):
        m = jnp.max(work, axis=0, keepdims=True)
        first = jnp.min(jnp.where(work == m, idx, float(n)), axis=0, keepdims=True)
        sel = idx == first
        rank = jnp.where(sel, float(k), rank)
        work = jnp.where(sel, -jnp.inf, work)
        vals = jnp.where(k_i == k, m, vals)
    return rank, vals


def _candidate_pieces(v1, v2):
    neg = -jnp.inf
    r8 = lax.broadcasted_iota(jnp.int32, (SUBLANES, v1.shape[1]), 0)
    pieces = [v1[0:1, :] + v2[0:8, :], v1[0:1, :] + v2[8:16, :], v1[1:2, :] + v2[0:8, :]]
    for a in range(2, 8):
        nb = PEER_TOPK // (a + 1)
        pieces.append(jnp.where(r8 < nb, v1[a:a + 1, :] + v2[0:8, :], neg))
    pieces.append(v1[8:16, :] + v2[0:1, :])
    return jnp.concatenate(pieces, axis=0)


def _peer_keys_body(x_ref, wq_ref, k1_ref, k2_ref, a_ref, cut_ref, b_ref, r2_ref):
    qp = jnp.dot(x_ref[...].astype(BF16), wq_ref[...], preferred_element_type=F32)
    for h in range(PEER_HEADS):
        q1 = qp[:, (2 * h) * PEER_HALF:(2 * h + 1) * PEER_HALF].astype(BF16)
        q2 = qp[:, (2 * h + 1) * PEER_HALF:(2 * h + 2) * PEER_HALF].astype(BF16)
        nt = (((1,), (1,)), ((), ()))
        s1 = lax.dot_general(k1_ref[h], q1, nt, preferred_element_type=F32)
        s2 = lax.dot_general(k2_ref[h], q2, nt, preferred_element_type=F32)
        r1, v1 = _top16_by_rows(s1)
        r2, v2 = _top16_by_rows(s2)
        cand = _candidate_pieces(v1, v2)
        rc, _ = _top16_by_rows(cand)
        chosen = rc < float(PEER_TOPK)
        top = cand[0:1, :]
        denom = jnp.sum(jnp.where(chosen, jnp.exp(cand - top), 0.0), axis=0, keepdims=True)
        cnt = jnp.where(chosen, 1.0, 0.0)
        n_of_a = [jnp.sum(cnt[0:16, :], axis=0, keepdims=True)]
        for a in range(1, 8):
            n_of_a.append(jnp.sum(cnt[8 + 8 * a:16 + 8 * a, :], axis=0, keepdims=True))
        for a in range(8, 16):
            n_of_a.append(cnt[64 + a:65 + a, :])
        cut = jnp.zeros_like(r1)
        for a in range(PEER_TOPK):
            cut = jnp.where(r1 == float(a), n_of_a[a], cut)
        a_ref[h] = jnp.exp(s1 - v1[0:1, :]) / denom
        cut_ref[h] = cut
        b_ref[h] = jnp.exp(s2 - v2[0:1, :])
        r2_ref[h] = r2


def _peer_keys(x, lw, tm_target=256):
    m = x.shape[0]
    tm = _pick_block(m, tm_target, LANES)
    spec3 = pl.BlockSpec((PEER_HEADS, PEER_N_KEYS, tm), lambda i: (0, 0, i))
    shape3 = jax.ShapeDtypeStruct((PEER_HEADS, PEER_N_KEYS, m), F32)
    return pl.pallas_call(
        _peer_keys_body,
        grid=(m // tm,),
        in_specs=[pl.BlockSpec((tm, D_MODEL), lambda i: (i, 0)),
                  pl.BlockSpec((D_MODEL, 2 * PEER_HEADS * PEER_HALF), lambda i: (0, 0)),
                  pl.BlockSpec((PEER_HEADS, PEER_N_KEYS, PEER_HALF), lambda i: (0, 0, 0)),
                  pl.BlockSpec((PEER_HEADS, PEER_N_KEYS, PEER_HALF), lambda i: (0, 0, 0))],
        out_specs=[spec3, spec3, spec3, spec3],
        out_shape=[shape3, shape3, shape3, shape3],
        compiler_params=_params(("parallel",)),
        name="peer_keys",
    )(x, lw["peer_wq"], lw["peer_k1"], lw["peer_k2"])


PEER_SUB = 256
PEER_ACC = 512
PEER_STEP = 1024
PEER_STRIP = 256


def _peer_experts_body(x_ref, u_ref, vt_ref, a_ref, cut_ref, b_ref, r2_ref, o_ref, acc_s, gate_s, coef_s):
    j = pl.program_id(1)
    tm = x_ref.shape[0]

    @pl.when(j == 0)
    def _():
        acc_s[...] = jnp.zeros_like(acc_s)

    for qq in range(PEER_STEP // PEER_N_KEYS):
        i1 = j * (PEER_STEP // PEER_N_KEYS) + qq
        rs = slice(qq * PEER_N_KEYS, (qq + 1) * PEER_N_KEYS)
        for st in range(tm // PEER_STRIP):
            cs = slice(st * PEER_STRIP, (st + 1) * PEER_STRIP)
            gate = None
            for h in range(PEER_HEADS):
                a = a_ref[h, pl.ds(i1, 1), cs]
                cut = cut_ref[h, pl.ds(i1, 1), cs]
                term = jnp.where(r2_ref[h, :, cs] < cut, b_ref[h, :, cs], 0.0) * a
                gate = term if gate is None else gate + term
            gate_s[rs, cs] = gate

    x16 = x_ref[...]
    for sb in range(PEER_STEP // PEER_SUB):
        rs = slice(sb * PEER_SUB, (sb + 1) * PEER_SUB)
        hid = lax.dot_general(u_ref[rs, :], x16, (((1,), (1,)), ((), ())), preferred_element_type=F32)
        act = 0.5 * hid * (1.0 + lax.erf(hid * (1.0 / math.sqrt(2.0))))
        coef_s[rs, :] = (gate_s[rs, :] * act).astype(BF16)
    for ab in range(PEER_STEP // PEER_ACC):
        rs = slice(ab * PEER_ACC, (ab + 1) * PEER_ACC)
        acc_s[...] = acc_s[...] + jnp.dot(vt_ref[:, rs], coef_s[rs, :], preferred_element_type=F32)

    @pl.when(j == pl.num_programs(1) - 1)
    def _():
        o_ref[...] = acc_s[...].T


def _peer_experts(x16, a, cut, b, r2, lw, tm_target=512):
    m = x16.shape[0]
    tm = _pick_block(m, tm_target, PEER_STRIP)
    assert tm % PEER_STRIP == 0
    n_exp = lw["peer_u"].shape[0]
    spec3 = pl.BlockSpec((PEER_HEADS, PEER_N_KEYS, tm), lambda i, j: (0, 0, i))
    return pl.pallas_call(
        _peer_experts_body,
        grid=(m // tm, n_exp // PEER_STEP),
        in_specs=[pl.BlockSpec((tm, D_MODEL), lambda i, j: (i, 0)),
                  pl.BlockSpec((PEER_STEP, D_MODEL), lambda i, j: (j, 0)),
                  pl.BlockSpec((D_MODEL, PEER_STEP), lambda i, j: (0, j)),
                  spec3, spec3, spec3, spec3],
        out_specs=pl.BlockSpec((tm, D_MODEL), lambda i, j: (i, 0)),
        out_shape=jax.ShapeDtypeStruct((m, D_MODEL), F32),
        scratch_shapes=[pltpu.VMEM((D_MODEL, tm), F32), pltpu.VMEM((PEER_STEP, tm), F32),
                        pltpu.VMEM((PEER_STEP, tm), BF16)],
        compiler_params=_params(("parallel", "arbitrary")),
        name="peer_experts",
    )(x16, lw["peer_u"], lw["peer_vt"], a, cut, b, r2)


def _post_body(x_ref, f_ref, p_ref, g_ref, b_ref, wg_ref, wp_ref, o_ref, *, alpha):
    x2 = _layer_norm(alpha * x_ref[...] + f_ref[...], g_ref[...], b_ref[...])
    gate = jax.nn.sigmoid(jnp.dot(x2.astype(BF16), wg_ref[...], preferred_element_type=F32))
    emb = jnp.dot(p_ref[...].astype(BF16), wp_ref[...], preferred_element_type=F32)
    o_ref[...] = x2 + gate * emb


def _post(x, ffn, p_emb, lw, alpha, tm_target=512):
    m = x.shape[0]
    tm = _pick_block(m, tm_target)
    ple = p_emb.shape[1]
    tok = lambda i: (i, 0)
    const = lambda i: (0, 0)
    return pl.pallas_call(
        functools.partial(_post_body, alpha=alpha),
        grid=(m // tm,),
        in_specs=[pl.BlockSpec((tm, D_MODEL), tok), pl.BlockSpec((tm, D_MODEL), tok),
                  pl.BlockSpec((tm, ple), tok),
                  pl.BlockSpec((1, D_MODEL), const), pl.BlockSpec((1, D_MODEL), const),
                  pl.BlockSpec((D_MODEL, D_MODEL), const), pl.BlockSpec((ple, D_MODEL), const)],
        out_specs=pl.BlockSpec((tm, D_MODEL), tok),
        out_shape=jax.ShapeDtypeStruct((m, D_MODEL), F32),
        compiler_params=_params(("parallel",)),
        name="ln2_ple",
    )(x, ffn, p_emb, lw["ln2_g"], lw["ln2_b"], lw["w_ple_gate"], lw["w_ple_proj"])


def _layer_weights(i, w_in, conv_w, conv_b, dt_bias, a_log, d_skip, ssm_norm_w, fox_f_bias, w_branch_a,
                   w_branch_b, w_out, ln1_g, ln1_b, ln2_g, ln2_b, peer_w_q, peer_sub_keys, peer_u, peer_v,
                   w_ple_gate, w_ple_proj):
    sizes = (SSM_D_INNER, SSM_CONV_DIM, SSM_HEADS, FOX_WIDTH, FOX_WIDTH, FOX_WIDTH, FOX_HEADS, D_MODEL, D_MODEL)
    offs = np.concatenate([[0], np.cumsum(sizes)])
    w = w_in[i]
    cols = {n: w[:, offs[j]:offs[j + 1]] for j, n in enumerate(("z", "xbc", "dt", "q", "k", "v", "f", "ga", "gb"))}
    small = jnp.concatenate([cols["dt"], cols["f"],
                             jnp.zeros((D_MODEL, LANES - SSM_HEADS - FOX_HEADS), F32)], axis=1)
    zpad = jnp.zeros((LANES - SSM_HEADS - FOX_HEADS,), F32)
    lw = {n: cols[n].astype(BF16) for n in ("z", "xbc", "k", "v", "ga", "gb")}
    lw["q_t"] = cols["q"].T.astype(BF16)
    lw["v_t"] = cols["v"].T.astype(BF16)
    lw["small"] = small.astype(BF16)
    lw["conv_w8"] = jnp.concatenate([conv_w[i], jnp.zeros((SUBLANES - SSM_CONV_W, SSM_CONV_DIM), F32)], axis=0)
    lw["conv_b"] = conv_b[i][None, :]
    lw["bias128"] = jnp.concatenate([dt_bias[i], fox_f_bias[i], zpad])[None, :]
    lw["aneg128"] = jnp.concatenate([-jnp.exp(a_log[i]), jnp.zeros((LANES - SSM_HEADS,), F32)])[None, :]
    lw["dskip_e"] = jnp.repeat(d_skip[i], SSM_HEAD_DIM)[None, :]
    lw["norm_w"] = ssm_norm_w[i][None, :]
    head_of_col = np.arange(SSM_D_INNER) // SSM_HEAD_DIM
    lw["expand"] = jnp.asarray(np.arange(LANES)[:, None] == head_of_col[None, :], F32)
    lw["tri"] = jnp.asarray(np.tril(np.ones((SSM_CHUNK, SSM_CHUNK), np.float32)))
    lw["w_a"] = w_branch_a[i].astype(BF16)
    lw["w_b"] = w_branch_b[i].astype(BF16)
    lw["w_out"] = w_out[i].astype(BF16)
    lw["ln1_g"], lw["ln1_b"] = ln1_g[i][None, :], ln1_b[i][None, :]
    lw["ln2_g"], lw["ln2_b"] = ln2_g[i][None, :], ln2_b[i][None, :]
    lw["peer_wq"] = peer_w_q[i].astype(BF16)
    lw["peer_k1"] = peer_sub_keys[i, 0].astype(BF16)
    lw["peer_k2"] = peer_sub_keys[i, 1].astype(BF16)
    lw["peer_u"] = peer_u[i].astype(BF16)
    lw["peer_vt"] = peer_v[i].T.astype(BF16)
    lw["w_ple_gate"] = w_ple_gate[i].astype(BF16)
    lw["w_ple_proj"] = w_ple_proj[i].astype(BF16)
    return lw


def _trunk_layer(x, p_emb, lw, conv_state, ssm_state, cache_k, cache_v, cache_lf_t, page_table, *, layer,
                 bp, lp, bs, ls, alpha):
    tp = bp * lp
    ts = bs * ls
    n_rows = x.shape[0]
    z = _matmul(x, lw["z"])
    xbc = _matmul(x, lw["xbc"])
    small = _matmul(x, lw["small"])
    q_t = _matmul_t(lw["q_t"], x, scale=FOX_SCALE * LOG2E)
    v_t = _matmul_t(lw["v_t"], x)
    k, k16 = _matmul(x, lw["k"], out_dtypes=(F32, BF16))
    v = _matmul(x, lw["v"])
    ga = _matmul(x, lw["ga"])
    gb = _matmul(x, lw["gb"])

    zero_conv = jnp.zeros((bp, SUBLANES, SSM_CONV_DIM), F32)
    zero_h = jnp.zeros((bp, SSM_D_INNER, SSM_D_STATE), F32)
    ya_p, lf_p, c_p, h_p = _ssd(xbc, z, small, zero_conv, zero_h, lw, row0=0, batch=bp, seqlen=lp)
    conv8 = jnp.concatenate([jnp.zeros((bs, SUBLANES - (SSM_CONV_W - 1), SSM_CONV_DIM), F32), conv_state], axis=1)
    ya_s, lf_s, c_s, h_s = _ssd(xbc, z, small, conv8, ssm_state.reshape(bs, SSM_D_INNER, SSM_D_STATE), lw,
                                row0=tp, batch=bs, seqlen=ls)
    logf_p = lf_p[:, F_LO:F_HI]
    logf_s = lf_s[:, F_LO:F_HI]
    cum_p = c_p[:, F_LO:F_HI]
    cum_s = c_s[:, F_LO:F_HI].reshape(bs, ls, FOX_HEADS)

    yb_p = _fox_prompt(q_t, k16, v_t, cum_p, batch=bp, seqlen=lp)
    page = cache_lf_t.shape[3]
    qs = q_t[:, tp:tp + ts].T.reshape(bs, ls, FOX_HEADS, FOX_HEAD_DIM)
    eye = jnp.eye(FOX_HEADS, dtype=BF16)
    qbd = (qs.transpose(0, 2, 1, 3)[:, :, :, None, :] * eye[None, :, None, :, None]).reshape(
        bs, FOX_HEADS * ls, FOX_WIDTH)
    knew = jnp.pad(k[tp:tp + ts].reshape(bs, ls, FOX_WIDTH), ((0, 0), (0, page - ls), (0, 0)))
    vnew = jnp.pad(v[tp:tp + ts].reshape(bs, ls, FOX_WIDTH), ((0, 0), (0, page - ls), (0, 0)))
    cnew_t = jnp.pad(cum_s.transpose(0, 2, 1), ((0, 0), (0, 0), (0, page - ls)))
    yb_s = _fox_sample(qbd, knew, vnew, cnew_t, cache_k, cache_v, cache_lf_t, page_table,
                       layer=layer, n_new=ls).reshape(ts, FOX_WIDTH)

    n_pad = n_rows - tp - ts
    ya = jnp.concatenate([ya_p, ya_s, jnp.zeros((n_pad, SSM_D_INNER), BF16)], axis=0)
    yb = jnp.concatenate([yb_p, yb_s, jnp.zeros((n_pad, FOX_WIDTH), BF16)], axis=0)
    x1, x1_16 = _merge(x, ya, yb, ga, gb, lw, alpha)
    a, cut, b, r2 = _peer_keys(x1_16, lw)
    ffn = _peer_experts(x1_16, a, cut, b, r2, lw)
    x2 = _post(x1, ffn, p_emb, lw, alpha)

    hshape = (FOX_HEADS, FOX_HEAD_DIM)
    sshape = (SSM_HEADS, SSM_HEAD_DIM, SSM_D_STATE)
    xbc_p = xbc[:tp].reshape(bp, lp, SSM_CONV_DIM)
    xbc_s = xbc[tp:tp + ts].reshape(bs, ls, SSM_CONV_DIM)
    prompt_out = (k[:tp].reshape(bp, lp, *hshape), v[:tp].reshape(bp, lp, *hshape),
                  logf_p.reshape(bp, lp, FOX_HEADS), h_p.reshape(bp, *sshape),
                  xbc_p[:, lp - (SSM_CONV_W - 1):])
    sample_out = (k[tp:tp + ts].reshape(bs, ls, *hshape), v[tp:tp + ts].reshape(bs, ls, *hshape),
                  logf_s.reshape(bs, ls, FOX_HEADS), h_s.reshape(bs, *sshape),
                  jnp.concatenate([conv_state, xbc_s], axis=1)[:, ls:])
    return x2, prompt_out, sample_out


def kernel(x_prompt, x_sample, cache_k, cache_v, cache_logf, state_ssm, state_conv, page_table, p_prompt, p_sample, w_in, conv_w, conv_b, dt_bias, a_log, d_skip, ssm_norm_w, fox_f_bias, w_branch_a, w_branch_b, w_out, ln1_g, ln1_b, ln2_g, ln2_b, peer_w_q, peer_sub_keys, peer_u, peer_v, w_ple_gate, w_ple_proj):
    depth = w_in.shape[0]
    bp, lp, _ = x_prompt.shape
    bs, ls, _ = x_sample.shape
    alpha = (2.0 * depth) ** 0.25
    clf_t = cache_logf.transpose(0, 1, 3, 2)
    n_pool, page = cache_k.shape[1], cache_k.shape[2]
    cache_k = cache_k.reshape(depth, n_pool, page * FOX_HEADS, FOX_HEAD_DIM)
    cache_v = cache_v.reshape(depth, n_pool, page * FOX_HEADS, FOX_HEAD_DIM)
    tp, ts = bp * lp, bs * ls
    n_pad = -(tp + ts) % TOKEN_TILE
    x = jnp.concatenate([x_prompt.reshape(tp, D_MODEL), x_sample.reshape(ts, D_MODEL),
                         jnp.zeros((n_pad, D_MODEL), F32)], axis=0)
    outs_p, outs_s = [], []
    for i in range(depth):
        lw = _layer_weights(i, w_in, conv_w, conv_b, dt_bias, a_log, d_skip, ssm_norm_w, fox_f_bias, w_branch_a,
                            w_branch_b, w_out, ln1_g, ln1_b, ln2_g, ln2_b, peer_w_q, peer_sub_keys, peer_u,
                            peer_v, w_ple_gate, w_ple_proj)
        ple = p_prompt.shape[-1]
        p_emb = jnp.concatenate([p_prompt[i].reshape(tp, ple), p_sample[i].reshape(ts, ple),
                                 jnp.zeros((n_pad, ple), F32)], axis=0)
        x, po, so = _trunk_layer(x, p_emb, lw, state_conv[i], state_ssm[i], cache_k, cache_v, clf_t, page_table,
                                 layer=i, bp=bp, lp=lp, bs=bs, ls=ls, alpha=alpha)
        outs_p.append(po)
        outs_s.append(so)
    stack = lambda outs, j: jnp.stack([o[j] for o in outs])
    return (x[:tp].reshape(bp, lp, D_MODEL), x[tp:tp + ts].reshape(bs, ls, D_MODEL),
            stack(outs_p, 0), stack(outs_p, 1), stack(outs_p, 2), stack(outs_p, 3), stack(outs_p, 4),
            stack(outs_s, 0), stack(outs_s, 1), stack(outs_s, 2), stack(outs_s, 3), stack(outs_s, 4))
```

```python
import functools
import math

import numpy as np
import jax
import jax.numpy as jnp
from jax import lax
from jax.experimental import pallas as pl
from jax.experimental.pallas import tpu as pltpu

F32 = jnp.float32
BF16 = jnp.bfloat16
HIGHEST = lax.Precision.HIGHEST

D_MODEL = 1024
SSM_D_INNER = 2048
SSM_HEAD_DIM = 64
SSM_HEADS = 32
SSM_GROUPS = 4
SSM_D_STATE = 128
SSM_CONV_W = 4
SSM_CONV_DIM = SSM_D_INNER + 2 * SSM_GROUPS * SSM_D_STATE
SSM_CHUNK = 128
FOX_HEAD_DIM = 128
FOX_HEADS = 8
FOX_WIDTH = FOX_HEADS * FOX_HEAD_DIM
FOX_SCALE = FOX_HEAD_DIM ** -0.5
PEER_HEADS = 8
PEER_N_KEYS = 128
PEER_TOPK = 16
PEER_HALF = 128
LN_EPS = 1e-5
RMS_EPS = 1e-6
LOG2E = math.log2(math.e)

LANES = 128
SUBLANES = 8
VMEM_LIMIT_BYTES = 56 * 1024 * 1024
TOKEN_TILE = 512

DT_LO, DT_HI = 0, SSM_HEADS
F_LO, F_HI = SSM_HEADS, SSM_HEADS + FOX_HEADS
NOT_RANKED = 99.0


def _pick_block(n, target, mult=SUBLANES):
    for d in range(min(n, target), 0, -1):
        if n % d == 0 and d % mult == 0:
            return d
    return n


def _params(sem):
    return pltpu.CompilerParams(dimension_semantics=sem, vmem_limit_bytes=VMEM_LIMIT_BYTES)


def _mm_body(x_ref, w_ref, *o_refs):
    acc = jnp.dot(x_ref[...].astype(BF16), w_ref[...], preferred_element_type=F32)
    for o_ref in o_refs:
        o_ref[...] = acc.astype(o_ref.dtype)


def _matmul(x, w, out_dtypes=(F32,), row0=0, nrows=None, tm_target=1280, tn_target=512):
    k = x.shape[1]
    m = x.shape[0] if nrows is None else nrows
    n = w.shape[1]
    tm = _pick_block(math.gcd(m, row0) if row0 else m, tm_target)
    tn = _pick_block(n, tn_target, LANES)
    blk0 = row0 // tm
    outs = pl.pallas_call(
        _mm_body,
        grid=(m // tm, n // tn),
        in_specs=[pl.BlockSpec((tm, k), lambda i, j: (blk0 + i, 0)),
                  pl.BlockSpec((k, tn), lambda i, j: (0, j))],
        out_specs=[pl.BlockSpec((tm, tn), lambda i, j: (i, j)) for _ in out_dtypes],
        out_shape=[jax.ShapeDtypeStruct((m, n), dt) for dt in out_dtypes],
        compiler_params=_params(("parallel", "parallel")),
        name="proj_matmul",
    )(x, w)
    return outs[0] if len(out_dtypes) == 1 else outs


def _mm_t_body(w_ref, x_ref, o_ref, *, scale):
    acc = lax.dot_general(w_ref[...], x_ref[...].astype(BF16), (((1,), (1,)), ((), ())),
                          preferred_element_type=F32)
    if scale is not None:
        acc = acc * scale
    o_ref[...] = acc.astype(o_ref.dtype)


def _matmul_t(w_t, x, out_dtype=BF16, scale=None, nrows=None, tm_target=1024, tn_target=512):
    n, k = w_t.shape
    m = x.shape[0] if nrows is None else nrows
    tm = _pick_block(m, tm_target, LANES)
    tn = _pick_block(n, tn_target)
    return pl.pallas_call(
        functools.partial(_mm_t_body, scale=scale),
        grid=(m // tm, n // tn),
        in_specs=[pl.BlockSpec((tn, k), lambda i, j: (j, 0)),
                  pl.BlockSpec((tm, k), lambda i, j: (i, 0))],
        out_specs=pl.BlockSpec((tn, tm), lambda i, j: (j, i)),
        out_shape=jax.ShapeDtypeStruct((n, m), out_dtype),
        compiler_params=_params(("parallel", "parallel")),
        name="proj_matmul_t",
    )(w_t, x)


def _ssd_body(xbc_ref, z_ref, sm_ref, cst_ref, h0_ref, cw_ref, cb_ref, bias_ref, aneg_ref,
              dsk_ref, nw_ref, e_ref, tri_ref,
              y_ref, lf_ref, c_ref, hfin_ref,
              xp_s, ht_s, cc_s, y_s, *, rows, nc):
    q = SSM_CHUNK
    ci = pl.program_id(1)

    @pl.when(ci == 0)
    def _():
        xp_s[0:SUBLANES, :] = cst_ref[0]
        ht_s[...] = h0_ref[0].T
        cc_s[...] = jnp.zeros_like(cc_s)

    if rows < q:
        pad = q - rows
        xp_s[SUBLANES:SUBLANES + rows, :] = xbc_ref[...]
        xp_s[SUBLANES + rows:SUBLANES + q, :] = jnp.zeros((pad, SSM_CONV_DIM), F32)
        z = jnp.concatenate([z_ref[...], jnp.zeros((pad, SSM_D_INNER), F32)], axis=0)
        sm = jnp.concatenate([sm_ref[...], jnp.zeros((pad, LANES), F32)], axis=0)
    else:
        xp_s[SUBLANES:SUBLANES + q, :] = xbc_ref[...]
        z = z_ref[...]
        sm = sm_ref[...]

    w = cw_ref[...]
    base = SUBLANES - (SSM_CONV_W - 1)
    conv = cb_ref[...]
    for t in range(SSM_CONV_W):
        conv = conv + xp_s[base + t:base + t + q, :] * w[t:t + 1, :]
    act = jax.nn.silu(conv)
    xp_s[0:SUBLANES, :] = xp_s[q:q + SUBLANES, :]

    lane = lax.broadcasted_iota(jnp.int32, (q, LANES), 1)
    row = lax.broadcasted_iota(jnp.int32, (q, LANES), 0)
    valid = row < rows
    sm = sm + bias_ref[...]
    dt = jnp.where(valid, jax.nn.softplus(sm), 0.0)
    logf = jnp.where(valid, jax.nn.log_sigmoid(sm), 0.0)
    is_dt = lane < DT_HI
    is_f = (lane >= F_LO) & (lane < F_HI)
    steps = jnp.where(is_dt, dt * aneg_ref[...], jnp.where(is_f, logf, 0.0))
    cum = jnp.dot(tri_ref[...], steps, precision=HIGHEST, preferred_element_type=F32)
    cfull = cum + cc_s[...]
    cc_s[...] = jnp.where(is_f[0:1, :], cfull[q - 1:q, :], 0.0)
    lf_ref[...] = logf[0:rows, :]
    c_ref[...] = cfull[0:rows, :]

    acum = cum
    ea = jnp.exp(acum)
    de = jnp.exp(acum[q - 1:q, :] - acum)
    e = e_ref[...]
    dt_e = jnp.dot(dt, e, precision=HIGHEST, preferred_element_type=F32)
    ea_e = jnp.dot(ea, e, precision=HIGHEST, preferred_element_type=F32)
    de_e = jnp.dot(de, e, precision=HIGHEST, preferred_element_type=F32)
    cd_e = ea_e[q - 1:q, :]

    xs = act[:, 0:SSM_D_INNER]
    xd = xs * dt_e
    acum_t = acum.T
    li = lax.broadcasted_iota(jnp.int32, (q, q), 0)
    si = lax.broadcasted_iota(jnp.int32, (q, q), 1)
    causal = li >= si
    half_lane = lax.broadcasted_iota(jnp.int32, (q, LANES), 1) < SSM_HEAD_DIM
    gw = SSM_D_INNER // SSM_GROUPS
    hpg = SSM_HEADS // SSM_GROUPS
    for g in range(SSM_GROUPS):
        bg = act[:, SSM_D_INNER + g * SSM_D_STATE:SSM_D_INNER + (g + 1) * SSM_D_STATE]
        cg = act[:, SSM_D_INNER + (SSM_GROUPS + g) * SSM_D_STATE:SSM_D_INNER + (SSM_GROUPS + g + 1) * SSM_D_STATE]
        bg16 = bg.astype(BF16)
        cg16 = cg.astype(BF16)
        cb = lax.dot_general(cg16, bg16, (((1,), (1,)), ((), ())), preferred_element_type=F32)
        for pr in range(hpg // 2):
            h0 = g * hpg + 2 * pr
            ms = []
            for h in (h0, h0 + 1):
                seg = acum[:, h:h + 1] - acum_t[h:h + 1, :]
                dec = jnp.exp(jnp.where(causal, seg, -jnp.inf))
                ms.append((cb * dec).astype(BF16))
            lhs = jnp.concatenate(ms, axis=1)
            xp = xd[:, h0 * SSM_HEAD_DIM:(h0 + 2) * SSM_HEAD_DIM]
            rhs = jnp.concatenate([jnp.where(half_lane, xp, 0.0), jnp.where(half_lane, 0.0, xp)],
                                  axis=0).astype(BF16)
            y_s[:, h0 * SSM_HEAD_DIM:(h0 + 2) * SSM_HEAD_DIM] = jnp.dot(
                lhs, rhs, preferred_element_type=F32)
        sl = slice(g * gw, (g + 1) * gw)
        ht_g = ht_s[:, sl]
        y_off = jnp.dot(cg16, ht_g.astype(BF16), preferred_element_type=F32) * ea_e[:, sl]
        y_s[:, sl] = y_s[:, sl] + y_off
        st = jnp.dot(bg.T.astype(BF16), (xd[:, sl] * de_e[:, sl]).astype(BF16), preferred_element_type=F32)
        ht_s[:, sl] = ht_g * cd_e[:, sl] + st

    y = y_s[...] + dsk_ref[...] * xs
    y = y * jax.nn.silu(z)
    nw = nw_ref[...]
    for g in range(SSM_GROUPS):
        sl = slice(g * gw, (g + 1) * gw)
        yg = y[:, sl]
        ms = jnp.sum(yg * yg, axis=1, keepdims=True) * (1.0 / gw)
        y_ref[:, sl] = (yg * lax.rsqrt(ms + RMS_EPS) * nw[:, sl])[0:rows, :].astype(y_ref.dtype)

    @pl.when(ci == nc - 1)
    def _():
        hfin_ref[0] = ht_s[...].T


def _ssd(xbc, z, small, conv_state8, h0, lw, *, row0, batch, seqlen):
    q = SSM_CHUNK
    rows = q if seqlen % q == 0 else seqlen
    assert rows % SUBLANES == 0 and rows <= q and row0 % rows == 0
    nc = seqlen // rows
    blk0 = row0 // rows
    tok = lambda b, c: (blk0 + b * nc + c, 0)
    out_tok = lambda b, c: (b * nc + c, 0)
    const2 = lambda b, c: (0, 0)
    n = batch * seqlen
    return pl.pallas_call(
        functools.partial(_ssd_body, rows=rows, nc=nc),
        grid=(batch, nc),
        in_specs=[pl.BlockSpec((rows, SSM_CONV_DIM), tok),
                  pl.BlockSpec((rows, SSM_D_INNER), tok),
                  pl.BlockSpec((rows, LANES), tok),
                  pl.BlockSpec((1, SUBLANES, SSM_CONV_DIM), lambda b, c: (b, 0, 0)),
                  pl.BlockSpec((1, SSM_D_INNER, SSM_D_STATE), lambda b, c: (b, 0, 0)),
                  pl.BlockSpec((SUBLANES, SSM_CONV_DIM), const2),
                  pl.BlockSpec((1, SSM_CONV_DIM), const2),
                  pl.BlockSpec((1, LANES), const2),
                  pl.BlockSpec((1, LANES), const2),
                  pl.BlockSpec((1, SSM_D_INNER), const2),
                  pl.BlockSpec((1, SSM_D_INNER), const2),
                  pl.BlockSpec((LANES, SSM_D_INNER), const2),
                  pl.BlockSpec((q, q), const2)],
        out_specs=[pl.BlockSpec((rows, SSM_D_INNER), out_tok),
                   pl.BlockSpec((rows, LANES), out_tok),
                   pl.BlockSpec((rows, LANES), out_tok),
                   pl.BlockSpec((1, SSM_D_INNER, SSM_D_STATE), lambda b, c: (b, 0, 0))],
        out_shape=[jax.ShapeDtypeStruct((n, SSM_D_INNER), BF16),
                   jax.ShapeDtypeStruct((n, LANES), F32),
                   jax.ShapeDtypeStruct((n, LANES), F32),
                   jax.ShapeDtypeStruct((batch, SSM_D_INNER, SSM_D_STATE), F32)],
        scratch_shapes=[pltpu.VMEM((q + SUBLANES, SSM_CONV_DIM), F32),
                        pltpu.VMEM((SSM_D_STATE, SSM_D_INNER), F32),
                        pltpu.VMEM((1, LANES), F32),
                        pltpu.VMEM((q, SSM_D_INNER), F32)],
        compiler_params=_params(("parallel", "arbitrary")),
        name="ssd_scan",
    )(xbc, z, small, conv_state8, h0, lw["conv_w8"], lw["conv_b"], lw["bias128"], lw["aneg128"],
      lw["dskip_e"], lw["norm_w"], lw["expand"], lw["tri"])


FLASH_Q_STRIP = 256
FLASH_K_TILE = 128


def _flash_body(qt_ref, k_ref, vt_ref, c_ref, o_ref, m_s, l_s, acc_s, *, tq):
    qi = pl.program_id(1)
    ki = pl.program_id(2)
    n_strip = tq // FLASH_Q_STRIP
    n_kt = tq // FLASH_K_TILE

    @pl.when(ki == 0)
    def _():
        m_s[...] = jnp.full_like(m_s, -jnp.inf)
        l_s[...] = jnp.zeros_like(l_s)
        acc_s[...] = jnp.zeros_like(acc_s)

    def block(diagonal):
        for h in range(FOX_HEADS):
            hs = slice(h * FOX_HEAD_DIM, (h + 1) * FOX_HEAD_DIM)
            ck = c_ref[:, h:h + 1] * LOG2E
            for st in range(n_strip):
                cs = slice(st * FLASH_Q_STRIP, (st + 1) * FLASH_Q_STRIP)
                q_t = qt_ref[hs, cs]
                m_prev = m_s[h:h + 1, cs]
                l_prev = l_s[h:h + 1, cs]
                for kt in range(n_kt):
                    k_lo, k_hi = kt * FLASH_K_TILE, (kt + 1) * FLASH_K_TILE
                    q_lo, q_hi = st * FLASH_Q_STRIP, (st + 1) * FLASH_Q_STRIP
                    if diagonal and k_lo > q_hi - 1:
                        continue
                    ks = slice(k_lo, k_hi)
                    s = jnp.dot(k_ref[ks, hs], q_t, preferred_element_type=F32)
                    s = s - ck[ks, :]
                    if diagonal and k_hi - 1 > q_lo:
                        key_i = k_lo + lax.broadcasted_iota(jnp.int32, s.shape, 0)
                        qry_i = q_lo + lax.broadcasted_iota(jnp.int32, s.shape, 1)
                        s = jnp.where(key_i <= qry_i, s, -jnp.inf)
                    m_new = jnp.maximum(m_prev, jnp.max(s, axis=0, keepdims=True))
                    alpha = jnp.exp2(m_prev - m_new)
                    p = jnp.exp2(s - m_new)
                    l_prev = alpha * l_prev + jnp.sum(p, axis=0, keepdims=True)
                    acc_s[hs, cs] = acc_s[hs, cs] * alpha + jnp.dot(vt_ref[hs, ks], p.astype(BF16),
                                                                    preferred_element_type=F32)
                    m_prev = m_new
                m_s[h:h + 1, cs] = m_prev
                l_s[h:h + 1, cs] = l_prev

    @pl.when(ki < qi)
    def _():
        block(False)

    @pl.when(ki == qi)
    def _():
        block(True)
        for h in range(FOX_HEADS):
            hs = slice(h * FOX_HEAD_DIM, (h + 1) * FOX_HEAD_DIM)
            o_ref[:, hs] = (acc_s[hs, :] / l_s[h:h + 1, :]).T.astype(o_ref.dtype)


def _fox_prompt(q_t, k16, v_t, c, *, batch, seqlen, tq_target=512):
    tq = _pick_block(seqlen, tq_target, FLASH_Q_STRIP)
    assert tq % FLASH_Q_STRIP == 0 and tq % FLASH_K_TILE == 0
    nq = seqlen // tq
    kv_blk = lambda b, i, j: b * nq + jnp.minimum(i, j)
    return pl.pallas_call(
        functools.partial(_flash_body, tq=tq),
        grid=(batch, nq, nq),
        in_specs=[pl.BlockSpec((FOX_WIDTH, tq), lambda b, i, j: (0, b * nq + i)),
                  pl.BlockSpec((tq, FOX_WIDTH), lambda b, i, j: (kv_blk(b, i, j), 0)),
                  pl.BlockSpec((FOX_WIDTH, tq), lambda b, i, j: (0, kv_blk(b, i, j))),
                  pl.BlockSpec((tq, FOX_HEADS), lambda b, i, j: (kv_blk(b, i, j), 0))],
        out_specs=pl.BlockSpec((tq, FOX_WIDTH), lambda b, i, j: (b * nq + i, 0)),
        out_shape=jax.ShapeDtypeStruct((batch * seqlen, FOX_WIDTH), BF16),
        scratch_shapes=[pltpu.VMEM((FOX_HEADS, tq), F32), pltpu.VMEM((FOX_HEADS, tq), F32),
                        pltpu.VMEM((FOX_WIDTH, tq), F32)],
        compiler_params=_params(("parallel", "parallel", "arbitrary")),
        name="fox_prompt",
    )(q_t, k16, v_t, c)


PAGES_PER_STEP = 4


def _paged_body(pt_ref, qbd_ref, knew_ref, vnew_ref, cnew_t_ref, *rest, n_new, n_pages, pps):
    del pt_ref
    kp_refs, vp_refs, lf_refs = rest[0:pps], rest[pps:2 * pps], rest[2 * pps:3 * pps]
    o_ref, m_s, l_s, acc_s, carry_s = rest[3 * pps:]
    step = pl.program_id(1)
    page = kp_refs[0].shape[2] // FOX_HEADS
    nrow = FOX_HEADS * n_new

    def rows_of_heads(x8):
        return jnp.concatenate([jnp.broadcast_to(x8[h:h + 1, :], (n_new, x8.shape[1]))
                                for h in range(FOX_HEADS)], axis=0)

    def update(k16, v16, key_bias, mask):
        s = lax.dot_general(qbd_ref[0], k16, (((1,), (1,)), ((), ())), preferred_element_type=F32)
        s = s + rows_of_heads(key_bias * LOG2E)
        if mask is not None:
            s = jnp.where(mask, s, -jnp.inf)
        m_prev = m_s[...]
        m_new = jnp.maximum(m_prev, jnp.max(s, axis=1, keepdims=True))
        alpha = jnp.exp2(m_prev - m_new)
        p = jnp.exp2(s - m_new)
        l_s[...] = alpha * l_s[...] + jnp.sum(p, axis=1, keepdims=True)
        m_s[...] = m_new
        acc_s[...] = acc_s[...] * alpha + jnp.dot(p.astype(BF16), v16, preferred_element_type=F32)

    def head_major(ref):
        return jnp.concatenate([ref[0, 0, pl.ds(h, page, stride=FOX_HEADS), :].astype(BF16)
                                for h in range(FOX_HEADS)], axis=1)

    @pl.when(step == 0)
    def _():
        m_s[...] = jnp.full_like(m_s, -jnp.inf)
        l_s[...] = jnp.zeros_like(l_s)
        acc_s[...] = jnp.zeros_like(acc_s)
        carry_s[...] = jnp.zeros_like(carry_s)
        rr = lax.broadcasted_iota(jnp.int32, (nrow, page), 0)
        kk = lax.broadcasted_iota(jnp.int32, (nrow, page), 1)
        update(knew_ref[0].astype(BF16), vnew_ref[0].astype(BF16), -cnew_t_ref[0], kk <= rr % n_new)

    @pl.when(step > 0)
    def _():
        ii = lax.broadcasted_iota(jnp.int32, (page, page), 0)
        jj = lax.broadcasted_iota(jnp.int32, (page, page), 1)
        later = jnp.where(ii > jj, 1.0, 0.0)
        carry = carry_s[...]
        biases = []
        for r in range(pps):
            lf_t = lf_refs[r][0, 0]
            biases.append(jnp.dot(lf_t, later, precision=HIGHEST, preferred_element_type=F32) + carry)
            carry = carry + jnp.sum(lf_t, axis=1, keepdims=True)
        carry_s[...] = carry
        update(jnp.concatenate([head_major(r) for r in kp_refs], axis=0),
               jnp.concatenate([head_major(r) for r in vp_refs], axis=0),
               jnp.concatenate(biases, axis=1), None)

    @pl.when(step == pl.num_programs(1) - 1)
    def _():
        for h in range(FOX_HEADS):
            rs = slice(h * n_new, (h + 1) * n_new)
            cs = slice(h * FOX_HEAD_DIM, (h + 1) * FOX_HEAD_DIM)
            o_ref[0, :, cs] = (acc_s[rs, cs] / l_s[rs, :]).astype(o_ref.dtype)


def _fox_sample(qbd, knew, vnew, cnew_t, cache_k, cache_v, cache_lf_t, page_table, *, layer, n_new):
    bsz, n_pages = page_table.shape
    page = cache_k.shape[2] // FOX_HEADS
    nrow = qbd.shape[1]
    pps = max(d for d in range(1, PAGES_PER_STEP + 1) if n_pages % d == 0)
    n_steps = n_pages // pps

    def page_of(r):
        return lambda b, s, pt: pt[b, n_pages - 1 - ((jnp.maximum(s, 1) - 1) * pps + r)]

    kv_maps = [(lambda b, s, pt, f=page_of(r): (layer, f(b, s, pt), 0, 0)) for r in range(pps)]
    lf_maps = kv_maps
    per_b = lambda b, s, pt: (b, 0, 0)
    kv_block = (1, 1, page * FOX_HEADS, FOX_HEAD_DIM)
    grid_spec = pltpu.PrefetchScalarGridSpec(
        num_scalar_prefetch=1,
        grid=(bsz, n_steps + 1),
        in_specs=[pl.BlockSpec((1, nrow, FOX_WIDTH), per_b),
                  pl.BlockSpec((1, page, FOX_WIDTH), per_b),
                  pl.BlockSpec((1, page, FOX_WIDTH), per_b),
                  pl.BlockSpec((1, FOX_HEADS, page), per_b)]
                 + [pl.BlockSpec(kv_block, m) for m in kv_maps]
                 + [pl.BlockSpec(kv_block, m) for m in kv_maps]
                 + [pl.BlockSpec((1, 1, FOX_HEADS, page), m) for m in lf_maps],
        out_specs=pl.BlockSpec((1, n_new, FOX_WIDTH), per_b),
        scratch_shapes=[pltpu.VMEM((nrow, 1), F32), pltpu.VMEM((nrow, 1), F32),
                        pltpu.VMEM((nrow, FOX_WIDTH), F32), pltpu.VMEM((FOX_HEADS, 1), F32)])
    return pl.pallas_call(
        functools.partial(_paged_body, n_new=n_new, n_pages=n_pages, pps=pps),
        grid_spec=grid_spec,
        out_shape=jax.ShapeDtypeStruct((bsz, n_new, FOX_WIDTH), BF16),
        compiler_params=_params(("parallel", "arbitrary")),
        name="fox_sample",
    )(page_table, qbd, knew, vnew, cnew_t, *([cache_k] * pps), *([cache_v] * pps), *([cache_lf_t] * pps))


def _layer_norm(r, g, b):
    mu = jnp.mean(r, axis=1, keepdims=True)
    d = r - mu
    var = jnp.mean(d * d, axis=1, keepdims=True)
    return d * lax.rsqrt(var + LN_EPS) * g + b


def _merge_body(x_ref, ya_ref, yb_ref, ga_ref, gb_ref, wa_ref, wb_ref, wo_ref, g_ref, b_ref, o_ref, o16_ref,
                *, alpha):
    a = jnp.dot(ya_ref[...], wa_ref[...], preferred_element_type=F32)
    b = jnp.dot(yb_ref[...], wb_ref[...], preferred_element_type=F32)
    merged = jax.nn.sigmoid(ga_ref[...]) * a + jax.nn.sigmoid(gb_ref[...]) * b
    r = alpha * x_ref[...] + jnp.dot(merged.astype(BF16), wo_ref[...], preferred_element_type=F32)
    y = _layer_norm(r, g_ref[...], b_ref[...])
    o_ref[...] = y
    o16_ref[...] = y.astype(BF16)


def _merge(x, ya, yb, ga, gb, lw, alpha, tm_target=256):
    m = x.shape[0]
    tm = _pick_block(m, tm_target)
    tok = lambda i: (i, 0)
    const = lambda i: (0, 0)
    return pl.pallas_call(
        functools.partial(_merge_body, alpha=alpha),
        grid=(m // tm,),
        in_specs=[pl.BlockSpec((tm, D_MODEL), tok), pl.BlockSpec((tm, SSM_D_INNER), tok),
                  pl.BlockSpec((tm, FOX_WIDTH), tok), pl.BlockSpec((tm, D_MODEL), tok),
                  pl.BlockSpec((tm, D_MODEL), tok),
                  pl.BlockSpec((SSM_D_INNER, D_MODEL), const), pl.BlockSpec((FOX_WIDTH, D_MODEL), const),
                  pl.BlockSpec((D_MODEL, D_MODEL), const),
                  pl.BlockSpec((1, D_MODEL), const), pl.BlockSpec((1, D_MODEL), const)],
        out_specs=[pl.BlockSpec((tm, D_MODEL), tok), pl.BlockSpec((tm, D_MODEL), tok)],
        out_shape=[jax.ShapeDtypeStruct((m, D_MODEL), F32), jax.ShapeDtypeStruct((m, D_MODEL), BF16)],
        compiler_params=_params(("parallel",)),
        name="merge_ln1",
    )(x, ya, yb, ga, gb, lw["w_a"], lw["w_b"], lw["w_out"], lw["ln1_g"], lw["ln1_b"])


def _top16_by_rows(s, break_ties):
    n = s.shape[0]
    idx = lax.broadcasted_iota(jnp.int32, s.shape, 0).astype(F32)
    k_i = lax.broadcasted_iota(jnp.int32, (PEER_TOPK, s.shape[1]), 0)
    rank = jnp.full(s.shape, NOT_RANKED, F32)
    vals = jnp.zeros((PEER_TOPK, s.shape[1]), F32)
    work = s
    for k in range(PEER_TOPK):
        m = jnp.max(work, axis=0, keepdims=True)
        sel = work == m
        if break_ties:
            first = jnp.min(jnp.where(sel, idx, float(n)), axis=0, keepdims=True)
            sel = idx == first
        rank = jnp.where(sel, float(k), rank)
        work = jnp.where(sel, -jnp.inf, work)
        vals = jnp.where(k_i == k, m, vals)
    return rank, vals


def _ranked_count(rank):
    return jnp.sum(jnp.where(rank < float(PEER_TOPK), 1.0, 0.0), axis=0, keepdims=True)


def _candidate_pieces(v1, v2):
    neg = -jnp.inf
    r8 = lax.broadcasted_iota(jnp.int32, (SUBLANES, v1.shape[1]), 0)
    pieces = [v1[0:1, :] + v2[0:8, :], v1[0:1, :] + v2[8:16, :], v1[1:2, :] + v2[0:8, :]]
    for a in range(2, 8):
        nb = PEER_TOPK // (a + 1)
        pieces.append(jnp.where(r8 < nb, v1[a:a + 1, :] + v2[0:8, :], neg))
    pieces.append(v1[8:16, :] + v2[0:1, :])
    return jnp.concatenate(pieces, axis=0)


def _peer_keys_body(x_ref, wq_ref, k1_ref, k2_ref, a_ref, cut_ref, b_ref, r2_ref, s_s):
    qp = jnp.dot(x_ref[...].astype(BF16), wq_ref[...], preferred_element_type=F32)
    nt = (((1,), (1,)), ((), ()))
    for h in range(PEER_HEADS):
        q1 = qp[:, (2 * h) * PEER_HALF:(2 * h + 1) * PEER_HALF].astype(BF16)
        q2 = qp[:, (2 * h + 1) * PEER_HALF:(2 * h + 2) * PEER_HALF].astype(BF16)
        s_s[0, h] = lax.dot_general(k1_ref[h], q1, nt, preferred_element_type=F32)
        s_s[1, h] = lax.dot_general(k2_ref[h], q2, nt, preferred_element_type=F32)

    clean = None
    for h in range(PEER_HEADS):
        ok = _retrieve_head(s_s[0, h], s_s[1, h], h, a_ref, cut_ref, b_ref, r2_ref, break_ties=False)
        clean = ok if clean is None else jnp.minimum(clean, ok)

    @pl.when(jnp.min(clean) < 1.0)
    def _():
        for h in range(PEER_HEADS):
            _retrieve_head(s_s[0, h], s_s[1, h], h, a_ref, cut_ref, b_ref, r2_ref, break_ties=True)


def _retrieve_head(s1, s2, h, a_ref, cut_ref, b_ref, r2_ref, *, break_ties):
    r1, v1 = _top16_by_rows(s1, break_ties)
    r2, v2 = _top16_by_rows(s2, break_ties)
    cand = _candidate_pieces(v1, v2)
    rc, _ = _top16_by_rows(cand, break_ties)
    chosen = rc < float(PEER_TOPK)
    top = cand[0:1, :]
    denom = jnp.sum(jnp.where(chosen, jnp.exp(cand - top), 0.0), axis=0, keepdims=True)
    cnt = jnp.where(chosen, 1.0, 0.0)
    n_of_a = [jnp.sum(cnt[0:16, :], axis=0, keepdims=True)]
    for a in range(1, 8):
        n_of_a.append(jnp.sum(cnt[8 + 8 * a:16 + 8 * a, :], axis=0, keepdims=True))
    for a in range(8, 16):
        n_of_a.append(cnt[64 + a:65 + a, :])
    cut = jnp.zeros_like(r1)
    for a in range(PEER_TOPK):
        cut = jnp.where(r1 == float(a), n_of_a[a], cut)
    a_ref[h] = jnp.exp(s1 - v1[0:1, :]) / denom
    cut_ref[h] = cut
    b_ref[h] = jnp.exp(s2 - v2[0:1, :])
    r2_ref[h] = r2
    total = float(PEER_TOPK)
    ok = (_ranked_count(r1) == total) & (_ranked_count(r2) == total) & (_ranked_count(rc) == total)
    return jnp.where(ok, 1.0, 0.0)


def _peer_keys(x, lw, tm_target=256):
    m = x.shape[0]
    tm = _pick_block(m, tm_target, LANES)
    spec3 = pl.BlockSpec((PEER_HEADS, PEER_N_KEYS, tm), lambda i: (0, 0, i))
    shape3 = jax.ShapeDtypeStruct((PEER_HEADS, PEER_N_KEYS, m), F32)
    return pl.pallas_call(
        _peer_keys_body,
        grid=(m // tm,),
        in_specs=[pl.BlockSpec((tm, D_MODEL), lambda i: (i, 0)),
                  pl.BlockSpec((D_MODEL, 2 * PEER_HEADS * PEER_HALF), lambda i: (0, 0)),
                  pl.BlockSpec((PEER_HEADS, PEER_N_KEYS, PEER_HALF), lambda i: (0, 0, 0)),
                  pl.BlockSpec((PEER_HEADS, PEER_N_KEYS, PEER_HALF), lambda i: (0, 0, 0))],
        out_specs=[spec3, spec3, spec3, spec3],
        out_shape=[shape3, shape3, shape3, shape3],
        scratch_shapes=[pltpu.VMEM((2, PEER_HEADS, PEER_N_KEYS, tm), F32)],
        compiler_params=_params(("parallel",)),
        name="peer_keys",
    )(x, lw["peer_wq"], lw["peer_k1"], lw["peer_k2"])


PEER_SUB = 256
PEER_ACC = 512
PEER_STEP = 1024
PEER_STRIP = 256


def _peer_experts_body(x_ref, u_ref, vt_ref, a_ref, cut_ref, b_ref, r2_ref, o_ref, acc_s, gate_s, coef_s):
    j = pl.program_id(1)
    tm = x_ref.shape[0]

    @pl.when(j == 0)
    def _():
        acc_s[...] = jnp.zeros_like(acc_s)

    for qq in range(PEER_STEP // PEER_N_KEYS):
        i1 = j * (PEER_STEP // PEER_N_KEYS) + qq
        rs = slice(qq * PEER_N_KEYS, (qq + 1) * PEER_N_KEYS)
        for st in range(tm // PEER_STRIP):
            cs = slice(st * PEER_STRIP, (st + 1) * PEER_STRIP)
            gate = None
            for h in range(PEER_HEADS):
                a = a_ref[h, pl.ds(i1, 1), cs]
                cut = cut_ref[h, pl.ds(i1, 1), cs]
                term = jnp.where(r2_ref[h, :, cs] < cut, b_ref[h, :, cs], 0.0) * a
                gate = term if gate is None else gate + term
            gate_s[rs, cs] = gate

    x16 = x_ref[...]
    for sb in range(PEER_STEP // PEER_SUB):
        rs = slice(sb * PEER_SUB, (sb + 1) * PEER_SUB)
        hid = lax.dot_general(u_ref[rs, :], x16, (((1,), (1,)), ((), ())), preferred_element_type=F32)
        act = 0.5 * hid * (1.0 + lax.erf(hid * (1.0 / math.sqrt(2.0))))
        coef_s[rs, :] = (gate_s[rs, :] * act).astype(BF16)
    for ab in range(PEER_STEP // PEER_ACC):
        rs = slice(ab * PEER_ACC, (ab + 1) * PEER_ACC)
        acc_s[...] = acc_s[...] + jnp.dot(vt_ref[:, rs], coef_s[rs, :], preferred_element_type=F32)

    @pl.when(j == pl.num_programs(1) - 1)
    def _():
        o_ref[...] = acc_s[...].T


def _peer_experts(x16, a, cut, b, r2, lw, tm_target=512):
    m = x16.shape[0]
    tm = _pick_block(m, tm_target, PEER_STRIP)
    assert tm % PEER_STRIP == 0
    n_exp = lw["peer_u"].shape[0]
    spec3 = pl.BlockSpec((PEER_HEADS, PEER_N_KEYS, tm), lambda i, j: (0, 0, i))
    return pl.pallas_call(
        _peer_experts_body,
        grid=(m // tm, n_exp // PEER_STEP),
        in_specs=[pl.BlockSpec((tm, D_MODEL), lambda i, j: (i, 0)),
                  pl.BlockSpec((PEER_STEP, D_MODEL), lambda i, j: (j, 0)),
                  pl.BlockSpec((D_MODEL, PEER_STEP), lambda i, j: (0, j)),
                  spec3, spec3, spec3, spec3],
        out_specs=pl.BlockSpec((tm, D_MODEL), lambda i, j: (i, 0)),
        out_shape=jax.ShapeDtypeStruct((m, D_MODEL), F32),
        scratch_shapes=[pltpu.VMEM((D_MODEL, tm), F32), pltpu.VMEM((PEER_STEP, tm), F32),
                        pltpu.VMEM((PEER_STEP, tm), BF16)],
        compiler_params=_params(("parallel", "arbitrary")),
        name="peer_experts",
    )(x16, lw["peer_u"], lw["peer_vt"], a, cut, b, r2)


def _post_body(x_ref, f_ref, p_ref, g_ref, b_ref, wg_ref, wp_ref, o_ref, *, alpha):
    x2 = _layer_norm(alpha * x_ref[...] + f_ref[...], g_ref[...], b_ref[...])
    gate = jax.nn.sigmoid(jnp.dot(x2.astype(BF16), wg_ref[...], preferred_element_type=F32))
    emb = jnp.dot(p_ref[...].astype(BF16), wp_ref[...], preferred_element_type=F32)
    o_ref[...] = x2 + gate * emb


def _post(x, ffn, p_emb, lw, alpha, tm_target=512):
    m = x.shape[0]
    tm = _pick_block(m, tm_target)
    ple = p_emb.shape[1]
    tok = lambda i: (i, 0)
    const = lambda i: (0, 0)
    return pl.pallas_call(
        functools.partial(_post_body, alpha=alpha),
        grid=(m // tm,),
        in_specs=[pl.BlockSpec((tm, D_MODEL), tok), pl.BlockSpec((tm, D_MODEL), tok),
                  pl.BlockSpec((tm, ple), tok),
                  pl.BlockSpec((1, D_MODEL), const), pl.BlockSpec((1, D_MODEL), const),
                  pl.BlockSpec((D_MODEL, D_MODEL), const), pl.BlockSpec((ple, D_MODEL), const)],
        out_specs=pl.BlockSpec((tm, D_MODEL), tok),
        out_shape=jax.ShapeDtypeStruct((m, D_MODEL), F32),
        compiler_params=_params(("parallel",)),
        name="ln2_ple",
    )(x, ffn, p_emb, lw["ln2_g"], lw["ln2_b"], lw["w_ple_gate"], lw["w_ple_proj"])


def _layer_weights(i, w_in, conv_w, conv_b, dt_bias, a_log, d_skip, ssm_norm_w, fox_f_bias, w_branch_a,
                   w_branch_b, w_out, ln1_g, ln1_b, ln2_g, ln2_b, peer_w_q, peer_sub_keys, peer_u, peer_v,
                   w_ple_gate, w_ple_proj):
    sizes = (SSM_D_INNER, SSM_CONV_DIM, SSM_HEADS, FOX_WIDTH, FOX_WIDTH, FOX_WIDTH, FOX_HEADS, D_MODEL, D_MODEL)
    offs = np.concatenate([[0], np.cumsum(sizes)])
    w = w_in[i]
    cols = {n: w[:, offs[j]:offs[j + 1]] for j, n in enumerate(("z", "xbc", "dt", "q", "k", "v", "f", "ga", "gb"))}
    small = jnp.concatenate([cols["dt"], cols["f"],
                             jnp.zeros((D_MODEL, LANES - SSM_HEADS - FOX_HEADS), F32)], axis=1)
    zpad = jnp.zeros((LANES - SSM_HEADS - FOX_HEADS,), F32)
    lw = {n: cols[n].astype(BF16) for n in ("z", "xbc", "k", "v", "ga", "gb")}
    lw["q_t"] = cols["q"].T.astype(BF16)
    lw["v_t"] = cols["v"].T.astype(BF16)
    lw["small"] = small.astype(BF16)
    lw["conv_w8"] = jnp.concatenate([conv_w[i], jnp.zeros((SUBLANES - SSM_CONV_W, SSM_CONV_DIM), F32)], axis=0)
    lw["conv_b"] = conv_b[i][None, :]
    lw["bias128"] = jnp.concatenate([dt_bias[i], fox_f_bias[i], zpad])[None, :]
    lw["aneg128"] = jnp.concatenate([-jnp.exp(a_log[i]), jnp.zeros((LANES - SSM_HEADS,), F32)])[None, :]
    lw["dskip_e"] = jnp.repeat(d_skip[i], SSM_HEAD_DIM)[None, :]
    lw["norm_w"] = ssm_norm_w[i][None, :]
    head_of_col = np.arange(SSM_D_INNER) // SSM_HEAD_DIM
    lw["expand"] = jnp.asarray(np.arange(LANES)[:, None] == head_of_col[None, :], F32)
    lw["tri"] = jnp.asarray(np.tril(np.ones((SSM_CHUNK, SSM_CHUNK), np.float32)))
    lw["w_a"] = w_branch_a[i].astype(BF16)
    lw["w_b"] = w_branch_b[i].astype(BF16)
    lw["w_out"] = w_out[i].astype(BF16)
    lw["ln1_g"], lw["ln1_b"] = ln1_g[i][None, :], ln1_b[i][None, :]
    lw["ln2_g"], lw["ln2_b"] = ln2_g[i][None, :], ln2_b[i][None, :]
    lw["peer_wq"] = peer_w_q[i].astype(BF16)
    lw["peer_k1"] = peer_sub_keys[i, 0].astype(BF16)
    lw["peer_k2"] = peer_sub_keys[i, 1].astype(BF16)
    lw["peer_u"] = peer_u[i].astype(BF16)
    lw["peer_vt"] = peer_v[i].T.astype(BF16)
    lw["w_ple_gate"] = w_ple_gate[i].astype(BF16)
    lw["w_ple_proj"] = w_ple_proj[i].astype(BF16)
    return lw


def _trunk_layer(x, p_emb, lw, conv_state, ssm_state, cache_k, cache_v, cache_lf_t, page_table, *, layer,
                 bp, lp, bs, ls, alpha):
    tp = bp * lp
    ts = bs * ls
    n_rows = x.shape[0]
    z = _matmul(x, lw["z"])
    xbc = _matmul(x, lw["xbc"])
    small = _matmul(x, lw["small"])
    q_t = _matmul_t(lw["q_t"], x, scale=FOX_SCALE * LOG2E)
    v_t = _matmul_t(lw["v_t"], x, nrows=tp)
    k_p, k16 = _matmul(x, lw["k"], out_dtypes=(F32, BF16), nrows=tp)
    v_p = _matmul(x, lw["v"], nrows=tp)
    k_s = _matmul(x, lw["k"], row0=tp, nrows=ts)
    v_s = _matmul(x, lw["v"], row0=tp, nrows=ts)
    ga = _matmul(x, lw["ga"])
    gb = _matmul(x, lw["gb"])

    zero_conv = jnp.zeros((bp, SUBLANES, SSM_CONV_DIM), F32)
    zero_h = jnp.zeros((bp, SSM_D_INNER, SSM_D_STATE), F32)
    ya_p, lf_p, c_p, h_p = _ssd(xbc, z, small, zero_conv, zero_h, lw, row0=0, batch=bp, seqlen=lp)
    conv8 = jnp.concatenate([jnp.zeros((bs, SUBLANES - (SSM_CONV_W - 1), SSM_CONV_DIM), F32), conv_state], axis=1)
    ya_s, lf_s, c_s, h_s = _ssd(xbc, z, small, conv8, ssm_state.reshape(bs, SSM_D_INNER, SSM_D_STATE), lw,
                                row0=tp, batch=bs, seqlen=ls)
    logf_p = lf_p[:, F_LO:F_HI]
    logf_s = lf_s[:, F_LO:F_HI]
    cum_p = c_p[:, F_LO:F_HI]
    cum_s = c_s[:, F_LO:F_HI].reshape(bs, ls, FOX_HEADS)

    yb_p = _fox_prompt(q_t, k16, v_t, cum_p, batch=bp, seqlen=lp)
    page = cache_lf_t.shape[3]
    qs = q_t[:, tp:tp + ts].T.reshape(bs, ls, FOX_HEADS, FOX_HEAD_DIM)
    eye = jnp.eye(FOX_HEADS, dtype=BF16)
    qbd = (qs.transpose(0, 2, 1, 3)[:, :, :, None, :] * eye[None, :, None, :, None]).reshape(
        bs, FOX_HEADS * ls, FOX_WIDTH)
    knew = jnp.pad(k_s.reshape(bs, ls, FOX_WIDTH), ((0, 0), (0, page - ls), (0, 0)))
    vnew = jnp.pad(v_s.reshape(bs, ls, FOX_WIDTH), ((0, 0), (0, page - ls), (0, 0)))
    cnew_t = jnp.pad(cum_s.transpose(0, 2, 1), ((0, 0), (0, 0), (0, page - ls)))
    yb_s = _fox_sample(qbd, knew, vnew, cnew_t, cache_k, cache_v, cache_lf_t, page_table,
                       layer=layer, n_new=ls).reshape(ts, FOX_WIDTH)

    n_pad = n_rows - tp - ts
    ya = jnp.concatenate([ya_p, ya_s, jnp.zeros((n_pad, SSM_D_INNER), BF16)], axis=0)
    yb = jnp.concatenate([yb_p, yb_s, jnp.zeros((n_pad, FOX_WIDTH), BF16)], axis=0)
    x1, x1_16 = _merge(x, ya, yb, ga, gb, lw, alpha)
    a, cut, b, r2 = _peer_keys(x1_16, lw)
    ffn = _peer_experts(x1_16, a, cut, b, r2, lw)
    x2 = _post(x1, ffn, p_emb, lw, alpha)

    hshape = (FOX_HEADS, FOX_HEAD_DIM)
    sshape = (SSM_HEADS, SSM_HEAD_DIM, SSM_D_STATE)
    xbc_p = xbc[:tp].reshape(bp, lp, SSM_CONV_DIM)
    xbc_s = xbc[tp:tp + ts].reshape(bs, ls, SSM_CONV_DIM)
    prompt_out = (k_p.reshape(bp, lp, *hshape), v_p.reshape(bp, lp, *hshape),
                  logf_p.reshape(bp, lp, FOX_HEADS), h_p.reshape(bp, *sshape),
                  xbc_p[:, lp - (SSM_CONV_W - 1):])
    sample_out = (k_s.reshape(bs, ls, *hshape), v_s.reshape(bs, ls, *hshape),
                  logf_s.reshape(bs, ls, FOX_HEADS), h_s.reshape(bs, *sshape),
                  jnp.concatenate([conv_state, xbc_s], axis=1)[:, ls:])
    return x2, prompt_out, sample_out


def kernel(x_prompt, x_sample, cache_k, cache_v, cache_logf, state_ssm, state_conv, page_table, p_prompt, p_sample, w_in, conv_w, conv_b, dt_bias, a_log, d_skip, ssm_norm_w, fox_f_bias, w_branch_a, w_branch_b, w_out, ln1_g, ln1_b, ln2_g, ln2_b, peer_w_q, peer_sub_keys, peer_u, peer_v, w_ple_gate, w_ple_proj):
    depth = w_in.shape[0]
    bp, lp, _ = x_prompt.shape
    bs, ls, _ = x_sample.shape
    alpha = (2.0 * depth) ** 0.25
    clf_t = cache_logf.transpose(0, 1, 3, 2)
    n_pool, page = cache_k.shape[1], cache_k.shape[2]
    cache_k = cache_k.reshape(depth, n_pool, page * FOX_HEADS, FOX_HEAD_DIM)
    cache_v = cache_v.reshape(depth, n_pool, page * FOX_HEADS, FOX_HEAD_DIM)
    tp, ts = bp * lp, bs * ls
    n_pad = -(tp + ts) % TOKEN_TILE
    x = jnp.concatenate([x_prompt.reshape(tp, D_MODEL), x_sample.reshape(ts, D_MODEL),
                         jnp.zeros((n_pad, D_MODEL), F32)], axis=0)
    outs_p, outs_s = [], []
    for i in range(depth):
        lw = _layer_weights(i, w_in, conv_w, conv_b, dt_bias, a_log, d_skip, ssm_norm_w, fox_f_bias, w_branch_a,
                            w_branch_b, w_out, ln1_g, ln1_b, ln2_g, ln2_b, peer_w_q, peer_sub_keys, peer_u,
                            peer_v, w_ple_gate, w_ple_proj)
        ple = p_prompt.shape[-1]
        p_emb = jnp.concatenate([p_prompt[i].reshape(tp, ple), p_sample[i].reshape(ts, ple),
                                 jnp.zeros((n_pad, ple), F32)], axis=0)
        x, po, so = _trunk_layer(x, p_emb, lw, state_conv[i], state_ssm[i], cache_k, cache_v, clf_t, page_table,
                                 layer=i, bp=bp, lp=lp, bs=bs, ls=ls, alpha=alpha)
        outs_p.append(po)
        outs_s.append(so)
    stack = lambda outs, j: jnp.stack([o[j] for o in outs])
    return (x[:tp].reshape(bp, lp, D_MODEL), x[tp:tp + ts].reshape(bs, ls, D_MODEL),
            stack(outs_p, 0), stack(outs_p, 1), stack(outs_p, 2), stack(outs_p, 3), stack(outs_p, 4),
            stack(outs_s, 0), stack(outs_s, 1), stack(outs_s, 2), stack(outs_s, 3), stack(outs_s, 4))
```

```python
import functools
import math

import numpy as np
import jax
import jax.numpy as jnp
from jax import lax
from jax.experimental import pallas as pl
from jax.experimental.pallas import tpu as pltpu

F32 = jnp.float32
BF16 = jnp.bfloat16
HIGHEST = lax.Precision.HIGHEST

D_MODEL = 1024
SSM_D_INNER = 2048
SSM_HEAD_DIM = 64
SSM_HEADS = 32
SSM_GROUPS = 4
SSM_D_STATE = 128
SSM_CONV_W = 4
SSM_CONV_DIM = SSM_D_INNER + 2 * SSM_GROUPS * SSM_D_STATE
SSM_CHUNK = 128
FOX_HEAD_DIM = 128
FOX_HEADS = 8
FOX_WIDTH = FOX_HEADS * FOX_HEAD_DIM
FOX_SCALE = FOX_HEAD_DIM ** -0.5
PEER_HEADS = 8
PEER_N_KEYS = 128
PEER_TOPK = 16
PEER_HALF = 128
LN_EPS = 1e-5
RMS_EPS = 1e-6
LOG2E = math.log2(math.e)

LANES = 128
SUBLANES = 8
VMEM_LIMIT_BYTES = 56 * 1024 * 1024
TOKEN_TILE = 512

DT_LO, DT_HI = 0, SSM_HEADS
F_LO, F_HI = SSM_HEADS, SSM_HEADS + FOX_HEADS
NOT_RANKED = 99.0


def _pick_block(n, target, mult=SUBLANES):
    for d in range(min(n, target), 0, -1):
        if n % d == 0 and d % mult == 0:
            return d
    return n


def _params(sem):
    return pltpu.CompilerParams(dimension_semantics=sem, vmem_limit_bytes=VMEM_LIMIT_BYTES)


def _mm_body(x_ref, w_ref, *o_refs):
    acc = jnp.dot(x_ref[...].astype(BF16), w_ref[...], preferred_element_type=F32)
    for o_ref in o_refs:
        o_ref[...] = acc.astype(o_ref.dtype)


def _matmul(x, w, out_dtypes=(F32,), row0=0, nrows=None, tm_target=1280, tn_target=512):
    k = x.shape[1]
    m = x.shape[0] if nrows is None else nrows
    n = w.shape[1]
    tm = _pick_block(math.gcd(m, row0) if row0 else m, tm_target)
    tn = _pick_block(n, tn_target, LANES)
    blk0 = row0 // tm
    outs = pl.pallas_call(
        _mm_body,
        grid=(m // tm, n // tn),
        in_specs=[pl.BlockSpec((tm, k), lambda i, j: (blk0 + i, 0)),
                  pl.BlockSpec((k, tn), lambda i, j: (0, j))],
        out_specs=[pl.BlockSpec((tm, tn), lambda i, j: (i, j)) for _ in out_dtypes],
        out_shape=[jax.ShapeDtypeStruct((m, n), dt) for dt in out_dtypes],
        compiler_params=_params(("parallel", "parallel")),
        name="proj_matmul",
    )(x, w)
    return outs[0] if len(out_dtypes) == 1 else outs


def _mm_t_body(w_ref, x_ref, o_ref, *, scale):
    acc = lax.dot_general(w_ref[...], x_ref[...].astype(BF16), (((1,), (1,)), ((), ())),
                          preferred_element_type=F32)
    if scale is not None:
        acc = acc * scale
    o_ref[...] = acc.astype(o_ref.dtype)


def _matmul_t(w_t, x, out_dtype=BF16, scale=None, nrows=None, tm_target=1024, tn_target=512):
    n, k = w_t.shape
    m = x.shape[0] if nrows is None else nrows
    tm = _pick_block(m, tm_target, LANES)
    tn = _pick_block(n, tn_target)
    return pl.pallas_call(
        functools.partial(_mm_t_body, scale=scale),
        grid=(m // tm, n // tn),
        in_specs=[pl.BlockSpec((tn, k), lambda i, j: (j, 0)),
                  pl.BlockSpec((tm, k), lambda i, j: (i, 0))],
        out_specs=pl.BlockSpec((tn, tm), lambda i, j: (j, i)),
        out_shape=jax.ShapeDtypeStruct((n, m), out_dtype),
        compiler_params=_params(("parallel", "parallel")),
        name="proj_matmul_t",
    )(w_t, x)


def _ssd_body(xbc_ref, z_ref, sm_ref, cst_ref, h0_ref, cw_ref, cb_ref, bias_ref, aneg_ref,
              dsk_ref, nw_ref, e_ref, tri_ref,
              y_ref, lf_ref, c_ref, hfin_ref,
              xp_s, ht_s, cc_s, y_s, *, rows, nc):
    q = SSM_CHUNK
    ci = pl.program_id(1)

    @pl.when(ci == 0)
    def _():
        xp_s[0:SUBLANES, :] = cst_ref[0]
        ht_s[...] = h0_ref[0].T
        cc_s[...] = jnp.zeros_like(cc_s)

    if rows < q:
        pad = q - rows
        xp_s[SUBLANES:SUBLANES + rows, :] = xbc_ref[...]
        xp_s[SUBLANES + rows:SUBLANES + q, :] = jnp.zeros((pad, SSM_CONV_DIM), F32)
        z = jnp.concatenate([z_ref[...], jnp.zeros((pad, SSM_D_INNER), F32)], axis=0)
        sm = jnp.concatenate([sm_ref[...], jnp.zeros((pad, LANES), F32)], axis=0)
    else:
        xp_s[SUBLANES:SUBLANES + q, :] = xbc_ref[...]
        z = z_ref[...]
        sm = sm_ref[...]

    w = cw_ref[...]
    base = SUBLANES - (SSM_CONV_W - 1)
    conv = cb_ref[...]
    for t in range(SSM_CONV_W):
        conv = conv + xp_s[base + t:base + t + q, :] * w[t:t + 1, :]
    act = jax.nn.silu(conv)
    xp_s[0:SUBLANES, :] = xp_s[q:q + SUBLANES, :]

    lane = lax.broadcasted_iota(jnp.int32, (q, LANES), 1)
    row = lax.broadcasted_iota(jnp.int32, (q, LANES), 0)
    valid = row < rows
    sm = sm + bias_ref[...]
    dt = jnp.where(valid, jax.nn.softplus(sm), 0.0)
    logf = jnp.where(valid, jax.nn.log_sigmoid(sm), 0.0)
    is_dt = lane < DT_HI
    is_f = (lane >= F_LO) & (lane < F_HI)
    steps = jnp.where(is_dt, dt * aneg_ref[...], jnp.where(is_f, logf, 0.0))
    cum = jnp.dot(tri_ref[...], steps, precision=HIGHEST, preferred_element_type=F32)
    cfull = cum + cc_s[...]
    cc_s[...] = jnp.where(is_f[0:1, :], cfull[q - 1:q, :], 0.0)
    lf_ref[...] = logf[0:rows, :]
    c_ref[...] = cfull[0:rows, :]

    acum = cum
    ea = jnp.exp(acum)
    de = jnp.exp(acum[q - 1:q, :] - acum)
    def per_head_to_channels(v):
        hi = v.astype(BF16)
        rest = v - hi.astype(F32)
        mid = rest.astype(BF16)
        lo = (rest - mid.astype(F32)).astype(BF16)
        return jnp.dot(jnp.concatenate([hi, mid, lo], axis=1), e_ref[...], preferred_element_type=F32)

    dt_e = per_head_to_channels(dt)
    ea_e = per_head_to_channels(ea)
    de_e = per_head_to_channels(de)
    cd_e = ea_e[q - 1:q, :]

    xs = act[:, 0:SSM_D_INNER]
    xd = xs * dt_e
    acum_t = acum.T
    li = lax.broadcasted_iota(jnp.int32, (q, q), 0)
    si = lax.broadcasted_iota(jnp.int32, (q, q), 1)
    causal = li >= si
    half_lane = lax.broadcasted_iota(jnp.int32, (q, LANES), 1) < SSM_HEAD_DIM
    gw = SSM_D_INNER // SSM_GROUPS
    hpg = SSM_HEADS // SSM_GROUPS
    for g in range(SSM_GROUPS):
        bg = act[:, SSM_D_INNER + g * SSM_D_STATE:SSM_D_INNER + (g + 1) * SSM_D_STATE]
        cg = act[:, SSM_D_INNER + (SSM_GROUPS + g) * SSM_D_STATE:SSM_D_INNER + (SSM_GROUPS + g + 1) * SSM_D_STATE]
        bg16 = bg.astype(BF16)
        cg16 = cg.astype(BF16)
        cb = lax.dot_general(cg16, bg16, (((1,), (1,)), ((), ())), preferred_element_type=F32)
        for pr in range(hpg // 2):
            h0 = g * hpg + 2 * pr
            ms = []
            for h in (h0, h0 + 1):
                seg = acum[:, h:h + 1] - acum_t[h:h + 1, :]
                dec = jnp.exp(jnp.where(causal, seg, -jnp.inf))
                ms.append((cb * dec).astype(BF16))
            lhs = jnp.concatenate(ms, axis=1)
            xp = xd[:, h0 * SSM_HEAD_DIM:(h0 + 2) * SSM_HEAD_DIM]
            rhs = jnp.concatenate([jnp.where(half_lane, xp, 0.0), jnp.where(half_lane, 0.0, xp)],
                                  axis=0).astype(BF16)
            y_s[:, h0 * SSM_HEAD_DIM:(h0 + 2) * SSM_HEAD_DIM] = jnp.dot(
                lhs, rhs, preferred_element_type=F32)
        sl = slice(g * gw, (g + 1) * gw)
        ht_g = ht_s[:, sl]
        y_off = jnp.dot(cg16, ht_g.astype(BF16), preferred_element_type=F32) * ea_e[:, sl]
        y_s[:, sl] = y_s[:, sl] + y_off
        st = jnp.dot(bg.T.astype(BF16), (xd[:, sl] * de_e[:, sl]).astype(BF16), preferred_element_type=F32)
        ht_s[:, sl] = ht_g * cd_e[:, sl] + st

    y = y_s[...] + dsk_ref[...] * xs
    y = y * jax.nn.silu(z)
    nw = nw_ref[...]
    for g in range(SSM_GROUPS):
        sl = slice(g * gw, (g + 1) * gw)
        yg = y[:, sl]
        ms = jnp.sum(yg * yg, axis=1, keepdims=True) * (1.0 / gw)
        y_ref[:, sl] = (yg * lax.rsqrt(ms + RMS_EPS) * nw[:, sl])[0:rows, :].astype(y_ref.dtype)

    @pl.when(ci == nc - 1)
    def _():
        hfin_ref[0] = ht_s[...].T


def _ssd(xbc, z, small, conv_state8, h0, lw, *, row0, batch, seqlen):
    q = SSM_CHUNK
    rows = q if seqlen % q == 0 else seqlen
    assert rows % SUBLANES == 0 and rows <= q and row0 % rows == 0
    nc = seqlen // rows
    blk0 = row0 // rows
    tok = lambda b, c: (blk0 + b * nc + c, 0)
    out_tok = lambda b, c: (b * nc + c, 0)
    const2 = lambda b, c: (0, 0)
    n = batch * seqlen
    return pl.pallas_call(
        functools.partial(_ssd_body, rows=rows, nc=nc),
        grid=(batch, nc),
        in_specs=[pl.BlockSpec((rows, SSM_CONV_DIM), tok),
                  pl.BlockSpec((rows, SSM_D_INNER), tok),
                  pl.BlockSpec((rows, LANES), tok),
                  pl.BlockSpec((1, SUBLANES, SSM_CONV_DIM), lambda b, c: (b, 0, 0)),
                  pl.BlockSpec((1, SSM_D_INNER, SSM_D_STATE), lambda b, c: (b, 0, 0)),
                  pl.BlockSpec((SUBLANES, SSM_CONV_DIM), const2),
                  pl.BlockSpec((1, SSM_CONV_DIM), const2),
                  pl.BlockSpec((1, LANES), const2),
                  pl.BlockSpec((1, LANES), const2),
                  pl.BlockSpec((1, SSM_D_INNER), const2),
                  pl.BlockSpec((1, SSM_D_INNER), const2),
                  pl.BlockSpec((3 * LANES, SSM_D_INNER), const2),
                  pl.BlockSpec((q, q), const2)],
        out_specs=[pl.BlockSpec((rows, SSM_D_INNER), out_tok),
                   pl.BlockSpec((rows, LANES), out_tok),
                   pl.BlockSpec((rows, LANES), out_tok),
                   pl.BlockSpec((1, SSM_D_INNER, SSM_D_STATE), lambda b, c: (b, 0, 0))],
        out_shape=[jax.ShapeDtypeStruct((n, SSM_D_INNER), BF16),
                   jax.ShapeDtypeStruct((n, LANES), F32),
                   jax.ShapeDtypeStruct((n, LANES), F32),
                   jax.ShapeDtypeStruct((batch, SSM_D_INNER, SSM_D_STATE), F32)],
        scratch_shapes=[pltpu.VMEM((q + SUBLANES, SSM_CONV_DIM), F32),
                        pltpu.VMEM((SSM_D_STATE, SSM_D_INNER), F32),
                        pltpu.VMEM((1, LANES), F32),
                        pltpu.VMEM((q, SSM_D_INNER), F32)],
        compiler_params=_params(("parallel", "arbitrary")),
        name="ssd_scan",
    )(xbc, z, small, conv_state8, h0, lw["conv_w8"], lw["conv_b"], lw["bias128"], lw["aneg128"],
      lw["dskip_e"], lw["norm_w"], lw["expand"], lw["tri"])


FLASH_Q_STRIP = 256
FLASH_K_TILE = 128


def _flash_body(qt_ref, k_ref, vt_ref, c_ref, o_ref, m_s, l_s, acc_s, *, tq):
    qi = pl.program_id(1)
    ki = pl.program_id(2)
    n_strip = tq // FLASH_Q_STRIP
    n_kt = tq // FLASH_K_TILE

    @pl.when(ki == 0)
    def _():
        m_s[...] = jnp.full_like(m_s, -jnp.inf)
        l_s[...] = jnp.zeros_like(l_s)
        acc_s[...] = jnp.zeros_like(acc_s)

    def block(diagonal):
        for h in range(FOX_HEADS):
            hs = slice(h * FOX_HEAD_DIM, (h + 1) * FOX_HEAD_DIM)
            ck = c_ref[:, h:h + 1] * LOG2E
            for st in range(n_strip):
                cs = slice(st * FLASH_Q_STRIP, (st + 1) * FLASH_Q_STRIP)
                q_t = qt_ref[hs, cs]
                m_prev = m_s[h:h + 1, cs]
                l_prev = l_s[h:h + 1, cs]
                for kt in range(n_kt):
                    k_lo, k_hi = kt * FLASH_K_TILE, (kt + 1) * FLASH_K_TILE
                    q_lo, q_hi = st * FLASH_Q_STRIP, (st + 1) * FLASH_Q_STRIP
                    if diagonal and k_lo > q_hi - 1:
                        continue
                    ks = slice(k_lo, k_hi)
                    s = jnp.dot(k_ref[ks, hs], q_t, preferred_element_type=F32)
                    s = s - ck[ks, :]
                    if diagonal and k_hi - 1 > q_lo:
                        key_i = k_lo + lax.broadcasted_iota(jnp.int32, s.shape, 0)
                        qry_i = q_lo + lax.broadcasted_iota(jnp.int32, s.shape, 1)
                        s = jnp.where(key_i <= qry_i, s, -jnp.inf)
                    m_new = jnp.maximum(m_prev, jnp.max(s, axis=0, keepdims=True))
                    alpha = jnp.exp2(m_prev - m_new)
                    p = jnp.exp2(s - m_new)
                    l_prev = alpha * l_prev + jnp.sum(p, axis=0, keepdims=True)
                    acc_s[hs, cs] = acc_s[hs, cs] * alpha + jnp.dot(vt_ref[hs, ks], p.astype(BF16),
                                                                    preferred_element_type=F32)
                    m_prev = m_new
                m_s[h:h + 1, cs] = m_prev
                l_s[h:h + 1, cs] = l_prev

    @pl.when(ki < qi)
    def _():
        block(False)

    @pl.when(ki == qi)
    def _():
        block(True)
        for h in range(FOX_HEADS):
            hs = slice(h * FOX_HEAD_DIM, (h + 1) * FOX_HEAD_DIM)
            o_ref[:, hs] = (acc_s[hs, :] / l_s[h:h + 1, :]).T.astype(o_ref.dtype)


def _fox_prompt(q_t, k16, v_t, c, *, batch, seqlen, tq_target=512):
    tq = _pick_block(seqlen, tq_target, FLASH_Q_STRIP)
    assert tq % FLASH_Q_STRIP == 0 and tq % FLASH_K_TILE == 0
    nq = seqlen // tq
    kv_blk = lambda b, i, j: b * nq + jnp.minimum(i, j)
    return pl.pallas_call(
        functools.partial(_flash_body, tq=tq),
        grid=(batch, nq, nq),
        in_specs=[pl.BlockSpec((FOX_WIDTH, tq), lambda b, i, j: (0, b * nq + i)),
                  pl.BlockSpec((tq, FOX_WIDTH), lambda b, i, j: (kv_blk(b, i, j), 0)),
                  pl.BlockSpec((FOX_WIDTH, tq), lambda b, i, j: (0, kv_blk(b, i, j))),
                  pl.BlockSpec((tq, FOX_HEADS), lambda b, i, j: (kv_blk(b, i, j), 0))],
        out_specs=pl.BlockSpec((tq, FOX_WIDTH), lambda b, i, j: (b * nq + i, 0)),
        out_shape=jax.ShapeDtypeStruct((batch * seqlen, FOX_WIDTH), BF16),
        scratch_shapes=[pltpu.VMEM((FOX_HEADS, tq), F32), pltpu.VMEM((FOX_HEADS, tq), F32),
                        pltpu.VMEM((FOX_WIDTH, tq), F32)],
        compiler_params=_params(("parallel", "parallel", "arbitrary")),
        name="fox_prompt",
    )(q_t, k16, v_t, c)


PAGES_PER_STEP = 8


def _paged_body(pt_ref, qbd_ref, knew_ref, vnew_ref, cnew_t_ref, *rest, n_new, n_pages, pps):
    del pt_ref
    kp_refs, vp_refs, lf_refs = rest[0:pps], rest[pps:2 * pps], rest[2 * pps:3 * pps]
    o_ref, m_s, l_s, acc_s, carry_s = rest[3 * pps:]
    step = pl.program_id(1)
    page = kp_refs[0].shape[2] // FOX_HEADS
    nrow = FOX_HEADS * n_new

    def rows_of_heads(x8):
        return jnp.concatenate([jnp.broadcast_to(x8[h:h + 1, :], (n_new, x8.shape[1]))
                                for h in range(FOX_HEADS)], axis=0)

    def update(k16, v16, key_bias, mask):
        s = lax.dot_general(qbd_ref[0], k16, (((1,), (1,)), ((), ())), preferred_element_type=F32)
        s = s + rows_of_heads(key_bias * LOG2E)
        if mask is not None:
            s = jnp.where(mask, s, -jnp.inf)
        m_prev = m_s[...]
        m_new = jnp.maximum(m_prev, jnp.max(s, axis=1, keepdims=True))
        alpha = jnp.exp2(m_prev - m_new)
        p = jnp.exp2(s - m_new)
        l_s[...] = alpha * l_s[...] + jnp.sum(p, axis=1, keepdims=True)
        m_s[...] = m_new
        acc_s[...] = acc_s[...] * alpha + jnp.dot(p.astype(BF16), v16, preferred_element_type=F32)

    def head_major(ref):
        return jnp.concatenate([ref[0, 0, pl.ds(h, page, stride=FOX_HEADS), :].astype(BF16)
                                for h in range(FOX_HEADS)], axis=1)

    @pl.when(step == 0)
    def _():
        m_s[...] = jnp.full_like(m_s, -jnp.inf)
        l_s[...] = jnp.zeros_like(l_s)
        acc_s[...] = jnp.zeros_like(acc_s)
        carry_s[...] = jnp.zeros_like(carry_s)
        rr = lax.broadcasted_iota(jnp.int32, (nrow, page), 0)
        kk = lax.broadcasted_iota(jnp.int32, (nrow, page), 1)
        update(knew_ref[0].astype(BF16), vnew_ref[0].astype(BF16), -cnew_t_ref[0], kk <= rr % n_new)

    @pl.when(step > 0)
    def _():
        ii = lax.broadcasted_iota(jnp.int32, (page, page), 0)
        jj = lax.broadcasted_iota(jnp.int32, (page, page), 1)
        later = jnp.where(ii > jj, 1.0, 0.0)
        carry = carry_s[...]
        biases = []
        for r in range(pps):
            lf_t = lf_refs[r][0, 0]
            biases.append(jnp.dot(lf_t, later, precision=HIGHEST, preferred_element_type=F32) + carry)
            carry = carry + jnp.sum(lf_t, axis=1, keepdims=True)
        carry_s[...] = carry
        update(jnp.concatenate([head_major(r) for r in kp_refs], axis=0),
               jnp.concatenate([head_major(r) for r in vp_refs], axis=0),
               jnp.concatenate(biases, axis=1), None)

    @pl.when(step == pl.num_programs(1) - 1)
    def _():
        for h in range(FOX_HEADS):
            rs = slice(h * n_new, (h + 1) * n_new)
            cs = slice(h * FOX_HEAD_DIM, (h + 1) * FOX_HEAD_DIM)
            o_ref[0, :, cs] = (acc_s[rs, cs] / l_s[rs, :]).astype(o_ref.dtype)


def _fox_sample(qbd, knew, vnew, cnew_t, cache_k, cache_v, cache_lf_t, page_table, *, layer, n_new):
    bsz, n_pages = page_table.shape
    page = cache_k.shape[2] // FOX_HEADS
    nrow = qbd.shape[1]
    pps = max(d for d in range(1, PAGES_PER_STEP + 1) if n_pages % d == 0)
    n_steps = n_pages // pps

    def page_of(r):
        return lambda b, s, pt: pt[b, n_pages - 1 - ((jnp.maximum(s, 1) - 1) * pps + r)]

    kv_maps = [(lambda b, s, pt, f=page_of(r): (layer, f(b, s, pt), 0, 0)) for r in range(pps)]
    lf_maps = kv_maps
    per_b = lambda b, s, pt: (b, 0, 0)
    kv_block = (1, 1, page * FOX_HEADS, FOX_HEAD_DIM)
    grid_spec = pltpu.PrefetchScalarGridSpec(
        num_scalar_prefetch=1,
        grid=(bsz, n_steps + 1),
        in_specs=[pl.BlockSpec((1, nrow, FOX_WIDTH), per_b),
                  pl.BlockSpec((1, page, FOX_WIDTH), per_b),
                  pl.BlockSpec((1, page, FOX_WIDTH), per_b),
                  pl.BlockSpec((1, FOX_HEADS, page), per_b)]
                 + [pl.BlockSpec(kv_block, m) for m in kv_maps]
                 + [pl.BlockSpec(kv_block, m) for m in kv_maps]
                 + [pl.BlockSpec((1, 1, FOX_HEADS, page), m) for m in lf_maps],
        out_specs=pl.BlockSpec((1, n_new, FOX_WIDTH), per_b),
        scratch_shapes=[pltpu.VMEM((nrow, 1), F32), pltpu.VMEM((nrow, 1), F32),
                        pltpu.VMEM((nrow, FOX_WIDTH), F32), pltpu.VMEM((FOX_HEADS, 1), F32)])
    return pl.pallas_call(
        functools.partial(_paged_body, n_new=n_new, n_pages=n_pages, pps=pps),
        grid_spec=grid_spec,
        out_shape=jax.ShapeDtypeStruct((bsz, n_new, FOX_WIDTH), BF16),
        compiler_params=_params(("parallel", "arbitrary")),
        name="fox_sample",
    )(page_table, qbd, knew, vnew, cnew_t, *([cache_k] * pps), *([cache_v] * pps), *([cache_lf_t] * pps))


def _layer_norm(r, g, b):
    mu = jnp.mean(r, axis=1, keepdims=True)
    d = r - mu
    var = jnp.mean(d * d, axis=1, keepdims=True)
    return d * lax.rsqrt(var + LN_EPS) * g + b


def _merge_body(x_ref, ya_ref, yb_ref, ga_ref, gb_ref, wa_ref, wb_ref, wo_ref, g_ref, b_ref, o_ref, o16_ref,
                *, alpha):
    a = jnp.dot(ya_ref[...], wa_ref[...], preferred_element_type=F32)
    b = jnp.dot(yb_ref[...], wb_ref[...], preferred_element_type=F32)
    merged = jax.nn.sigmoid(ga_ref[...]) * a + jax.nn.sigmoid(gb_ref[...]) * b
    r = alpha * x_ref[...] + jnp.dot(merged.astype(BF16), wo_ref[...], preferred_element_type=F32)
    y = _layer_norm(r, g_ref[...], b_ref[...])
    o_ref[...] = y
    o16_ref[...] = y.astype(BF16)


def _merge(x, ya, yb, ga, gb, lw, alpha, tm_target=256):
    m = x.shape[0]
    tm = _pick_block(m, tm_target)
    tok = lambda i: (i, 0)
    const = lambda i: (0, 0)
    return pl.pallas_call(
        functools.partial(_merge_body, alpha=alpha),
        grid=(m // tm,),
        in_specs=[pl.BlockSpec((tm, D_MODEL), tok), pl.BlockSpec((tm, SSM_D_INNER), tok),
                  pl.BlockSpec((tm, FOX_WIDTH), tok), pl.BlockSpec((tm, D_MODEL), tok),
                  pl.BlockSpec((tm, D_MODEL), tok),
                  pl.BlockSpec((SSM_D_INNER, D_MODEL), const), pl.BlockSpec((FOX_WIDTH, D_MODEL), const),
                  pl.BlockSpec((D_MODEL, D_MODEL), const),
                  pl.BlockSpec((1, D_MODEL), const), pl.BlockSpec((1, D_MODEL), const)],
        out_specs=[pl.BlockSpec((tm, D_MODEL), tok), pl.BlockSpec((tm, D_MODEL), tok)],
        out_shape=[jax.ShapeDtypeStruct((m, D_MODEL), F32), jax.ShapeDtypeStruct((m, D_MODEL), BF16)],
        compiler_params=_params(("parallel",)),
        name="merge_ln1",
    )(x, ya, yb, ga, gb, lw["w_a"], lw["w_b"], lw["w_out"], lw["ln1_g"], lw["ln1_b"])


def _top16_by_rows(s, break_ties):
    n = s.shape[0]
    idx = lax.broadcasted_iota(jnp.int32, s.shape, 0).astype(F32)
    k_i = lax.broadcasted_iota(jnp.int32, (PEER_TOPK, s.shape[1]), 0)
    rank = jnp.full(s.shape, NOT_RANKED, F32)
    vals = jnp.zeros((PEER_TOPK, s.shape[1]), F32)
    work = s
    for k in range(PEER_TOPK):
        m = jnp.max(work, axis=0, keepdims=True)
        sel = work == m
        if break_ties:
            first = jnp.min(jnp.where(sel, idx, float(n)), axis=0, keepdims=True)
            sel = idx == first
        rank = jnp.where(sel, float(k), rank)
        work = jnp.where(sel, -jnp.inf, work)
        vals = jnp.where(k_i == k, m, vals)
    return rank, vals


def _ranked_count(rank):
    return jnp.sum(jnp.where(rank < float(PEER_TOPK), 1.0, 0.0), axis=0, keepdims=True)


def _candidate_pieces(v1, v2):
    neg = -jnp.inf
    r8 = lax.broadcasted_iota(jnp.int32, (SUBLANES, v1.shape[1]), 0)
    pieces = [v1[0:1, :] + v2[0:8, :], v1[0:1, :] + v2[8:16, :], v1[1:2, :] + v2[0:8, :]]
    for a in range(2, 8):
        nb = PEER_TOPK // (a + 1)
        pieces.append(jnp.where(r8 < nb, v1[a:a + 1, :] + v2[0:8, :], neg))
    pieces.append(v1[8:16, :] + v2[0:1, :])
    return jnp.concatenate(pieces, axis=0)


def _peer_keys_body(x_ref, wq_ref, k1_ref, k2_ref, a_ref, cut_ref, b_ref, r2_ref, s_s):
    qp = jnp.dot(x_ref[...].astype(BF16), wq_ref[...], preferred_element_type=F32)
    nt = (((1,), (1,)), ((), ()))
    for h in range(PEER_HEADS):
        q1 = qp[:, (2 * h) * PEER_HALF:(2 * h + 1) * PEER_HALF].astype(BF16)
        q2 = qp[:, (2 * h + 1) * PEER_HALF:(2 * h + 2) * PEER_HALF].astype(BF16)
        s_s[0, h] = lax.dot_general(k1_ref[h], q1, nt, preferred_element_type=F32)
        s_s[1, h] = lax.dot_general(k2_ref[h], q2, nt, preferred_element_type=F32)

    clean = None
    for h in range(PEER_HEADS):
        ok = _retrieve_head(s_s[0, h], s_s[1, h], h, a_ref, cut_ref, b_ref, r2_ref, break_ties=False)
        clean = ok if clean is None else jnp.minimum(clean, ok)

    @pl.when(jnp.min(clean) < 1.0)
    def _():
        for h in range(PEER_HEADS):
            _retrieve_head(s_s[0, h], s_s[1, h], h, a_ref, cut_ref, b_ref, r2_ref, break_ties=True)


def _retrieve_head(s1, s2, h, a_ref, cut_ref, b_ref, r2_ref, *, break_ties):
    r1, v1 = _top16_by_rows(s1, break_ties)
    r2, v2 = _top16_by_rows(s2, break_ties)
    cand = _candidate_pieces(v1, v2)
    rc, _ = _top16_by_rows(cand, break_ties)
    chosen = rc < float(PEER_TOPK)
    top = cand[0:1, :]
    denom = jnp.sum(jnp.where(chosen, jnp.exp(cand - top), 0.0), axis=0, keepdims=True)
    cnt = jnp.where(chosen, 1.0, 0.0)
    n_of_a = [jnp.sum(cnt[0:16, :], axis=0, keepdims=True)]
    for a in range(1, 8):
        n_of_a.append(jnp.sum(cnt[8 + 8 * a:16 + 8 * a, :], axis=0, keepdims=True))
    for a in range(8, 16):
        n_of_a.append(cnt[64 + a:65 + a, :])
    cut = jnp.zeros_like(r1)
    for a in range(PEER_TOPK):
        cut = jnp.where(r1 == float(a), n_of_a[a], cut)
    a_ref[h] = jnp.exp(s1 - v1[0:1, :]) / denom
    cut_ref[h] = cut
    b_ref[h] = jnp.exp(s2 - v2[0:1, :])
    r2_ref[h] = r2
    total = float(PEER_TOPK)
    ok = (_ranked_count(r1) == total) & (_ranked_count(r2) == total) & (_ranked_count(rc) == total)
    return jnp.where(ok, 1.0, 0.0)


def _peer_keys(x, lw, tm_target=256):
    m = x.shape[0]
    tm = _pick_block(m, tm_target, LANES)
    spec3 = pl.BlockSpec((PEER_HEADS, PEER_N_KEYS, tm), lambda i: (0, 0, i))
    shape3 = jax.ShapeDtypeStruct((PEER_HEADS, PEER_N_KEYS, m), F32)
    return pl.pallas_call(
        _peer_keys_body,
        grid=(m // tm,),
        in_specs=[pl.BlockSpec((tm, D_MODEL), lambda i: (i, 0)),
                  pl.BlockSpec((D_MODEL, 2 * PEER_HEADS * PEER_HALF), lambda i: (0, 0)),
                  pl.BlockSpec((PEER_HEADS, PEER_N_KEYS, PEER_HALF), lambda i: (0, 0, 0)),
                  pl.BlockSpec((PEER_HEADS, PEER_N_KEYS, PEER_HALF), lambda i: (0, 0, 0))],
        out_specs=[spec3, spec3, spec3, spec3],
        out_shape=[shape3, shape3, shape3, shape3],
        scratch_shapes=[pltpu.VMEM((2, PEER_HEADS, PEER_N_KEYS, tm), F32)],
        compiler_params=_params(("parallel",)),
        name="peer_keys",
    )(x, lw["peer_wq"], lw["peer_k1"], lw["peer_k2"])


PEER_SUB = 256
PEER_ACC = 512
PEER_STEP = 1024
PEER_STRIP = 256


def _peer_experts_body(x_ref, u_ref, vt_ref, a_ref, cut_ref, b_ref, r2_ref, o_ref, acc_s, gate_s, coef_s):
    j = pl.program_id(1)
    tm = x_ref.shape[0]

    @pl.when(j == 0)
    def _():
        acc_s[...] = jnp.zeros_like(acc_s)

    for qq in range(PEER_STEP // PEER_N_KEYS):
        i1 = j * (PEER_STEP // PEER_N_KEYS) + qq
        rs = slice(qq * PEER_N_KEYS, (qq + 1) * PEER_N_KEYS)
        for st in range(tm // PEER_STRIP):
            cs = slice(st * PEER_STRIP, (st + 1) * PEER_STRIP)
            gate = None
            for h in range(PEER_HEADS):
                a = a_ref[h, pl.ds(i1, 1), cs]
                cut = cut_ref[h, pl.ds(i1, 1), cs]
                term = jnp.where(r2_ref[h, :, cs] < cut, b_ref[h, :, cs], 0.0) * a
                gate = term if gate is None else gate + term
            gate_s[rs, cs] = gate

    x16 = x_ref[...]
    for sb in range(PEER_STEP // PEER_SUB):
        rs = slice(sb * PEER_SUB, (sb + 1) * PEER_SUB)
        hid = lax.dot_general(u_ref[rs, :], x16, (((1,), (1,)), ((), ())), preferred_element_type=F32)
        act = 0.5 * hid * (1.0 + lax.erf(hid * (1.0 / math.sqrt(2.0))))
        coef_s[rs, :] = (gate_s[rs, :] * act).astype(BF16)
    for ab in range(PEER_STEP // PEER_ACC):
        rs = slice(ab * PEER_ACC, (ab + 1) * PEER_ACC)
        acc_s[...] = acc_s[...] + jnp.dot(vt_ref[:, rs], coef_s[rs, :], preferred_element_type=F32)

    @pl.when(j == pl.num_programs(1) - 1)
    def _():
        o_ref[...] = acc_s[...].T


def _peer_experts(x16, a, cut, b, r2, lw, tm_target=512):
    m = x16.shape[0]
    tm = _pick_block(m, tm_target, PEER_STRIP)
    assert tm % PEER_STRIP == 0
    n_exp = lw["peer_u"].shape[0]
    spec3 = pl.BlockSpec((PEER_HEADS, PEER_N_KEYS, tm), lambda i, j: (0, 0, i))
    return pl.pallas_call(
        _peer_experts_body,
        grid=(m // tm, n_exp // PEER_STEP),
        in_specs=[pl.BlockSpec((tm, D_MODEL), lambda i, j: (i, 0)),
                  pl.BlockSpec((PEER_STEP, D_MODEL), lambda i, j: (j, 0)),
                  pl.BlockSpec((D_MODEL, PEER_STEP), lambda i, j: (0, j)),
                  spec3, spec3, spec3, spec3],
        out_specs=pl.BlockSpec((tm, D_MODEL), lambda i, j: (i, 0)),
        out_shape=jax.ShapeDtypeStruct((m, D_MODEL), F32),
        scratch_shapes=[pltpu.VMEM((D_MODEL, tm), F32), pltpu.VMEM((PEER_STEP, tm), F32),
                        pltpu.VMEM((PEER_STEP, tm), BF16)],
        compiler_params=_params(("parallel", "arbitrary")),
        name="peer_experts",
    )(x16, lw["peer_u"], lw["peer_vt"], a, cut, b, r2)


def _post_body(x_ref, f_ref, p_ref, g_ref, b_ref, wg_ref, wp_ref, o_ref, *, alpha):
    x2 = _layer_norm(alpha * x_ref[...] + f_ref[...], g_ref[...], b_ref[...])
    gate = jax.nn.sigmoid(jnp.dot(x2.astype(BF16), wg_ref[...], preferred_element_type=F32))
    emb = jnp.dot(p_ref[...].astype(BF16), wp_ref[...], preferred_element_type=F32)
    o_ref[...] = x2 + gate * emb


def _post(x, ffn, p_emb, lw, alpha, row0=0, nrows=None, tm_target=512):
    m = x.shape[0] if nrows is None else nrows
    tm = _pick_block(math.gcd(m, row0) if row0 else m, tm_target)
    blk0 = row0 // tm
    ple = p_emb.shape[1]
    tok = lambda i: (blk0 + i, 0)
    const = lambda i: (0, 0)
    return pl.pallas_call(
        functools.partial(_post_body, alpha=alpha),
        grid=(m // tm,),
        in_specs=[pl.BlockSpec((tm, D_MODEL), tok), pl.BlockSpec((tm, D_MODEL), tok),
                  pl.BlockSpec((tm, ple), tok),
                  pl.BlockSpec((1, D_MODEL), const), pl.BlockSpec((1, D_MODEL), const),
                  pl.BlockSpec((D_MODEL, D_MODEL), const), pl.BlockSpec((ple, D_MODEL), const)],
        out_specs=pl.BlockSpec((tm, D_MODEL), lambda i: (i, 0)),
        out_shape=jax.ShapeDtypeStruct((m, D_MODEL), F32),
        compiler_params=_params(("parallel",)),
        name="ln2_ple",
    )(x, ffn, p_emb, lw["ln2_g"], lw["ln2_b"], lw["w_ple_gate"], lw["w_ple_proj"])


def _layer_weights(i, w_in, conv_w, conv_b, dt_bias, a_log, d_skip, ssm_norm_w, fox_f_bias, w_branch_a,
                   w_branch_b, w_out, ln1_g, ln1_b, ln2_g, ln2_b, peer_w_q, peer_sub_keys, peer_u, peer_v,
                   w_ple_gate, w_ple_proj):
    sizes = (SSM_D_INNER, SSM_CONV_DIM, SSM_HEADS, FOX_WIDTH, FOX_WIDTH, FOX_WIDTH, FOX_HEADS, D_MODEL, D_MODEL)
    offs = np.concatenate([[0], np.cumsum(sizes)])
    w = w_in[i]
    cols = {n: w[:, offs[j]:offs[j + 1]] for j, n in enumerate(("z", "xbc", "dt", "q", "k", "v", "f", "ga", "gb"))}
    small = jnp.concatenate([cols["dt"], cols["f"],
                             jnp.zeros((D_MODEL, LANES - SSM_HEADS - FOX_HEADS), F32)], axis=1)
    zpad = jnp.zeros((LANES - SSM_HEADS - FOX_HEADS,), F32)
    lw = {n: cols[n].astype(BF16) for n in ("z", "xbc", "k", "v", "ga", "gb")}
    lw["q_t"] = cols["q"].T.astype(BF16)
    lw["v_t"] = cols["v"].T.astype(BF16)
    lw["small"] = small.astype(BF16)
    lw["conv_w8"] = jnp.concatenate([conv_w[i], jnp.zeros((SUBLANES - SSM_CONV_W, SSM_CONV_DIM), F32)], axis=0)
    lw["conv_b"] = conv_b[i][None, :]
    lw["bias128"] = jnp.concatenate([dt_bias[i], fox_f_bias[i], zpad])[None, :]
    lw["aneg128"] = jnp.concatenate([-jnp.exp(a_log[i]), jnp.zeros((LANES - SSM_HEADS,), F32)])[None, :]
    lw["dskip_e"] = jnp.repeat(d_skip[i], SSM_HEAD_DIM)[None, :]
    lw["norm_w"] = ssm_norm_w[i][None, :]
    head_of_col = np.arange(SSM_D_INNER) // SSM_HEAD_DIM
    expand = np.arange(LANES)[:, None] == head_of_col[None, :]
    lw["expand"] = jnp.asarray(np.concatenate([expand] * 3, axis=0), BF16)
    lw["tri"] = jnp.asarray(np.tril(np.ones((SSM_CHUNK, SSM_CHUNK), np.float32)))
    lw["w_a"] = w_branch_a[i].astype(BF16)
    lw["w_b"] = w_branch_b[i].astype(BF16)
    lw["w_out"] = w_out[i].astype(BF16)
    lw["ln1_g"], lw["ln1_b"] = ln1_g[i][None, :], ln1_b[i][None, :]
    lw["ln2_g"], lw["ln2_b"] = ln2_g[i][None, :], ln2_b[i][None, :]
    lw["peer_wq"] = peer_w_q[i].astype(BF16)
    lw["peer_k1"] = peer_sub_keys[i, 0].astype(BF16)
    lw["peer_k2"] = peer_sub_keys[i, 1].astype(BF16)
    lw["peer_u"] = peer_u[i].astype(BF16)
    lw["peer_vt"] = peer_v[i].T.astype(BF16)
    lw["w_ple_gate"] = w_ple_gate[i].astype(BF16)
    lw["w_ple_proj"] = w_ple_proj[i].astype(BF16)
    return lw


def _trunk_layer(x, p_emb, lw, conv_state, ssm_state, cache_k, cache_v, cache_lf_t, page_table, *, layer,
                 bp, lp, bs, ls, alpha, last):
    tp = bp * lp
    ts = bs * ls
    n_rows = x.shape[0]
    z = _matmul(x, lw["z"])
    xbc = _matmul(x, lw["xbc"])
    small = _matmul(x, lw["small"])
    q_t = _matmul_t(lw["q_t"], x, scale=FOX_SCALE * LOG2E)
    v_t = _matmul_t(lw["v_t"], x, nrows=tp)
    k_p, k16 = _matmul(x, lw["k"], out_dtypes=(F32, BF16), nrows=tp)
    v_p = _matmul(x, lw["v"], nrows=tp)
    k_s = _matmul(x, lw["k"], row0=tp, nrows=ts)
    v_s = _matmul(x, lw["v"], row0=tp, nrows=ts)
    ga = _matmul(x, lw["ga"])
    gb = _matmul(x, lw["gb"])

    zero_conv = jnp.zeros((bp, SUBLANES, SSM_CONV_DIM), F32)
    zero_h = jnp.zeros((bp, SSM_D_INNER, SSM_D_STATE), F32)
    ya_p, lf_p, c_p, h_p = _ssd(xbc, z, small, zero_conv, zero_h, lw, row0=0, batch=bp, seqlen=lp)
    conv8 = jnp.concatenate([jnp.zeros((bs, SUBLANES - (SSM_CONV_W - 1), SSM_CONV_DIM), F32), conv_state], axis=1)
    ya_s, lf_s, c_s, h_s = _ssd(xbc, z, small, conv8, ssm_state.reshape(bs, SSM_D_INNER, SSM_D_STATE), lw,
                                row0=tp, batch=bs, seqlen=ls)
    logf_p = lf_p[:, F_LO:F_HI]
    logf_s = lf_s[:, F_LO:F_HI]
    cum_p = c_p[:, F_LO:F_HI]
    cum_s = c_s[:, F_LO:F_HI].reshape(bs, ls, FOX_HEADS)

    yb_p = _fox_prompt(q_t, k16, v_t, cum_p, batch=bp, seqlen=lp)
    page = cache_lf_t.shape[3]
    qs = q_t[:, tp:tp + ts].T.reshape(bs, ls, FOX_HEADS, FOX_HEAD_DIM)
    eye = jnp.eye(FOX_HEADS, dtype=BF16)
    qbd = (qs.transpose(0, 2, 1, 3)[:, :, :, None, :] * eye[None, :, None, :, None]).reshape(
        bs, FOX_HEADS * ls, FOX_WIDTH)
    knew = jnp.pad(k_s.reshape(bs, ls, FOX_WIDTH), ((0, 0), (0, page - ls), (0, 0)))
    vnew = jnp.pad(v_s.reshape(bs, ls, FOX_WIDTH), ((0, 0), (0, page - ls), (0, 0)))
    cnew_t = jnp.pad(cum_s.transpose(0, 2, 1), ((0, 0), (0, 0), (0, page - ls)))
    yb_s = _fox_sample(qbd, knew, vnew, cnew_t, cache_k, cache_v, cache_lf_t, page_table,
                       layer=layer, n_new=ls).reshape(ts, FOX_WIDTH)

    n_pad = n_rows - tp - ts
    ya = jnp.concatenate([ya_p, ya_s, jnp.zeros((n_pad, SSM_D_INNER), BF16)], axis=0)
    yb = jnp.concatenate([yb_p, yb_s, jnp.zeros((n_pad, FOX_WIDTH), BF16)], axis=0)
    x1, x1_16 = _merge(x, ya, yb, ga, gb, lw, alpha)
    a, cut, b, r2 = _peer_keys(x1_16, lw)
    ffn = _peer_experts(x1_16, a, cut, b, r2, lw)
    if last:
        x2 = (_post(x1, ffn, p_emb, lw, alpha, nrows=tp), _post(x1, ffn, p_emb, lw, alpha, row0=tp, nrows=ts))
    else:
        x2 = _post(x1, ffn, p_emb, lw, alpha)

    hshape = (FOX_HEADS, FOX_HEAD_DIM)
    sshape = (SSM_HEADS, SSM_HEAD_DIM, SSM_D_STATE)
    xbc_p = xbc[:tp].reshape(bp, lp, SSM_CONV_DIM)
    xbc_s = xbc[tp:tp + ts].reshape(bs, ls, SSM_CONV_DIM)
    prompt_out = (k_p.reshape(bp, lp, *hshape), v_p.reshape(bp, lp, *hshape),
                  logf_p.reshape(bp, lp, FOX_HEADS), h_p.reshape(bp, *sshape),
                  xbc_p[:, lp - (SSM_CONV_W - 1):])
    sample_out = (k_s.reshape(bs, ls, *hshape), v_s.reshape(bs, ls, *hshape),
                  logf_s.reshape(bs, ls, FOX_HEADS), h_s.reshape(bs, *sshape),
                  jnp.concatenate([conv_state, xbc_s], axis=1)[:, ls:])
    return x2, prompt_out, sample_out


def kernel(x_prompt, x_sample, cache_k, cache_v, cache_logf, state_ssm, state_conv, page_table, p_prompt, p_sample, w_in, conv_w, conv_b, dt_bias, a_log, d_skip, ssm_norm_w, fox_f_bias, w_branch_a, w_branch_b, w_out, ln1_g, ln1_b, ln2_g, ln2_b, peer_w_q, peer_sub_keys, peer_u, peer_v, w_ple_gate, w_ple_proj):
    depth = w_in.shape[0]
    bp, lp, _ = x_prompt.shape
    bs, ls, _ = x_sample.shape
    alpha = (2.0 * depth) ** 0.25
    clf_t = cache_logf.transpose(0, 1, 3, 2)
    n_pool, page = cache_k.shape[1], cache_k.shape[2]
    cache_k = cache_k.reshape(depth, n_pool, page * FOX_HEADS, FOX_HEAD_DIM)
    cache_v = cache_v.reshape(depth, n_pool, page * FOX_HEADS, FOX_HEAD_DIM)
    tp, ts = bp * lp, bs * ls
    n_pad = -(tp + ts) % TOKEN_TILE
    x = jnp.concatenate([x_prompt.reshape(tp, D_MODEL), x_sample.reshape(ts, D_MODEL),
                         jnp.zeros((n_pad, D_MODEL), F32)], axis=0)
    outs_p, outs_s = [], []
    for i in range(depth):
        lw = _layer_weights(i, w_in, conv_w, conv_b, dt_bias, a_log, d_skip, ssm_norm_w, fox_f_bias, w_branch_a,
                            w_branch_b, w_out, ln1_g, ln1_b, ln2_g, ln2_b, peer_w_q, peer_sub_keys, peer_u,
                            peer_v, w_ple_gate, w_ple_proj)
        ple = p_prompt.shape[-1]
        p_emb = jnp.concatenate([p_prompt[i].reshape(tp, ple), p_sample[i].reshape(ts, ple),
                                 jnp.zeros((n_pad, ple), F32)], axis=0)
        x, po, so = _trunk_layer(x, p_emb, lw, state_conv[i], state_ssm[i], cache_k, cache_v, clf_t, page_table,
                                 layer=i, bp=bp, lp=lp, bs=bs, ls=ls, alpha=alpha, last=i == depth - 1)
        outs_p.append(po)
        outs_s.append(so)
    stack = lambda outs, j: jnp.stack([o[j] for o in outs])
    y_prompt, y_sample = x
    return (y_prompt.reshape(bp, lp, D_MODEL), y_sample.reshape(bs, ls, D_MODEL),
            stack(outs_p, 0), stack(outs_p, 1), stack(outs_p, 2), stack(outs_p, 3), stack(outs_p, 4),
            stack(outs_s, 0), stack(outs_s, 1), stack(outs_s, 2), stack(outs_s, 3), stack(outs_s, 4))
```

```python
import functools
import math

import numpy as np
import jax
import jax.numpy as jnp
from jax import lax
from jax.experimental import pallas as pl
from jax.experimental.pallas import tpu as pltpu

F32 = jnp.float32
BF16 = jnp.bfloat16
HIGHEST = lax.Precision.HIGHEST

D_MODEL = 1024
SSM_D_INNER = 2048
SSM_HEAD_DIM = 64
SSM_HEADS = 32
SSM_GROUPS = 4
SSM_D_STATE = 128
SSM_CONV_W = 4
SSM_CONV_DIM = SSM_D_INNER + 2 * SSM_GROUPS * SSM_D_STATE
SSM_CHUNK = 128
FOX_HEAD_DIM = 128
FOX_HEADS = 8
FOX_WIDTH = FOX_HEADS * FOX_HEAD_DIM
FOX_SCALE = FOX_HEAD_DIM ** -0.5
PEER_HEADS = 8
PEER_N_KEYS = 128
PEER_TOPK = 16
PEER_HALF = 128
LN_EPS = 1e-5
RMS_EPS = 1e-6
LOG2E = math.log2(math.e)

LANES = 128
SUBLANES = 8
VMEM_LIMIT_BYTES = 56 * 1024 * 1024
TOKEN_TILE = 512

DT_LO, DT_HI = 0, SSM_HEADS
F_LO, F_HI = SSM_HEADS, SSM_HEADS + FOX_HEADS
NOT_RANKED = 99.0


def _pick_block(n, target, mult=SUBLANES):
    for d in range(min(n, target), 0, -1):
        if n % d == 0 and d % mult == 0:
            return d
    return n


def _params(sem):
    return pltpu.CompilerParams(dimension_semantics=sem, vmem_limit_bytes=VMEM_LIMIT_BYTES)


def _mm_body(x_ref, w_ref, *o_refs):
    acc = jnp.dot(x_ref[...].astype(BF16), w_ref[...], preferred_element_type=F32)
    for o_ref in o_refs:
        o_ref[...] = acc.astype(o_ref.dtype)


def _matmul(x, w, out_dtypes=(F32,), row0=0, nrows=None, tm_target=1280, tn_target=512):
    k = x.shape[1]
    m = x.shape[0] if nrows is None else nrows
    n = w.shape[1]
    tm = _pick_block(math.gcd(m, row0) if row0 else m, tm_target)
    tn = _pick_block(n, tn_target, LANES)
    blk0 = row0 // tm
    outs = pl.pallas_call(
        _mm_body,
        grid=(m // tm, n // tn),
        in_specs=[pl.BlockSpec((tm, k), lambda i, j: (blk0 + i, 0)),
                  pl.BlockSpec((k, tn), lambda i, j: (0, j))],
        out_specs=[pl.BlockSpec((tm, tn), lambda i, j: (i, j)) for _ in out_dtypes],
        out_shape=[jax.ShapeDtypeStruct((m, n), dt) for dt in out_dtypes],
        compiler_params=_params(("parallel", "parallel")),
        name="proj_matmul",
    )(x, w)
    return outs[0] if len(out_dtypes) == 1 else outs


def _mm_t_body(w_ref, x_ref, o_ref, *, scale):
    acc = lax.dot_general(w_ref[...], x_ref[...].astype(BF16), (((1,), (1,)), ((), ())),
                          preferred_element_type=F32)
    if scale is not None:
        acc = acc * scale
    o_ref[...] = acc.astype(o_ref.dtype)


def _matmul_t(w_t, x, out_dtype=BF16, scale=None, nrows=None, tm_target=1024, tn_target=512):
    n, k = w_t.shape
    m = x.shape[0] if nrows is None else nrows
    tm = _pick_block(m, tm_target, LANES)
    tn = _pick_block(n, tn_target)
    return pl.pallas_call(
        functools.partial(_mm_t_body, scale=scale),
        grid=(m // tm, n // tn),
        in_specs=[pl.BlockSpec((tn, k), lambda i, j: (j, 0)),
                  pl.BlockSpec((tm, k), lambda i, j: (i, 0))],
        out_specs=pl.BlockSpec((tn, tm), lambda i, j: (j, i)),
        out_shape=jax.ShapeDtypeStruct((n, m), out_dtype),
        compiler_params=_params(("parallel", "parallel")),
        name="proj_matmul_t",
    )(w_t, x)


def _ssd_body(xbc_ref, z_ref, sm_ref, cst_ref, h0_ref, cw_ref, cb_ref, bias_ref, aneg_ref,
              dsk_ref, nw_ref, e_ref, tri_ref,
              y_ref, lf_ref, c_ref, hfin_ref,
              xp_s, ht_s, cc_s, y_s, *, rows, nc):
    q = SSM_CHUNK
    ci = pl.program_id(1)

    @pl.when(ci == 0)
    def _():
        xp_s[0:SUBLANES, :] = cst_ref[0]
        ht_s[...] = h0_ref[0].T
        cc_s[...] = jnp.zeros_like(cc_s)

    if rows < q:
        pad = q - rows
        xp_s[SUBLANES:SUBLANES + rows, :] = xbc_ref[...]
        xp_s[SUBLANES + rows:SUBLANES + q, :] = jnp.zeros((pad, SSM_CONV_DIM), F32)
        z = jnp.concatenate([z_ref[...], jnp.zeros((pad, SSM_D_INNER), F32)], axis=0)
        sm = jnp.concatenate([sm_ref[...], jnp.zeros((pad, LANES), F32)], axis=0)
    else:
        xp_s[SUBLANES:SUBLANES + q, :] = xbc_ref[...]
        z = z_ref[...]
        sm = sm_ref[...]

    w = cw_ref[...]
    base = SUBLANES - (SSM_CONV_W - 1)
    conv = cb_ref[...]
    for t in range(SSM_CONV_W):
        conv = conv + xp_s[base + t:base + t + q, :] * w[t:t + 1, :]
    act = jax.nn.silu(conv)
    xp_s[0:SUBLANES, :] = xp_s[q:q + SUBLANES, :]

    lane = lax.broadcasted_iota(jnp.int32, (q, LANES), 1)
    row = lax.broadcasted_iota(jnp.int32, (q, LANES), 0)
    valid = row < rows
    sm = sm + bias_ref[...]
    dt = jnp.where(valid, jax.nn.softplus(sm), 0.0)
    logf = jnp.where(valid, jax.nn.log_sigmoid(sm), 0.0)
    is_dt = lane < DT_HI
    is_f = (lane >= F_LO) & (lane < F_HI)
    steps = jnp.where(is_dt, dt * aneg_ref[...], jnp.where(is_f, logf, 0.0))
    cum = jnp.dot(tri_ref[...], steps, precision=HIGHEST, preferred_element_type=F32)
    cfull = cum + cc_s[...]
    cc_s[...] = jnp.where(is_f[0:1, :], cfull[q - 1:q, :], 0.0)
    lf_ref[...] = logf[0:rows, :]
    c_ref[...] = cfull[0:rows, :]

    acum = cum
    ea = jnp.exp(acum)
    de = jnp.exp(acum[q - 1:q, :] - acum)
    def per_head_to_channels(v):
        hi = v.astype(BF16)
        rest = v - hi.astype(F32)
        mid = rest.astype(BF16)
        lo = (rest - mid.astype(F32)).astype(BF16)
        return jnp.dot(jnp.concatenate([hi, mid, lo], axis=1), e_ref[...], preferred_element_type=F32)

    dt_e = per_head_to_channels(dt)
    ea_e = per_head_to_channels(ea)
    de_e = per_head_to_channels(de)
    cd_e = ea_e[q - 1:q, :]

    xs = act[:, 0:SSM_D_INNER]
    xd = xs * dt_e
    acum_t = acum.T
    li = lax.broadcasted_iota(jnp.int32, (q, q), 0)
    si = lax.broadcasted_iota(jnp.int32, (q, q), 1)
    causal = li >= si
    half_lane = lax.broadcasted_iota(jnp.int32, (q, LANES), 1) < SSM_HEAD_DIM
    gw = SSM_D_INNER // SSM_GROUPS
    hpg = SSM_HEADS // SSM_GROUPS
    for g in range(SSM_GROUPS):
        bg = act[:, SSM_D_INNER + g * SSM_D_STATE:SSM_D_INNER + (g + 1) * SSM_D_STATE]
        cg = act[:, SSM_D_INNER + (SSM_GROUPS + g) * SSM_D_STATE:SSM_D_INNER + (SSM_GROUPS + g + 1) * SSM_D_STATE]
        bg16 = bg.astype(BF16)
        cg16 = cg.astype(BF16)
        cb = lax.dot_general(cg16, bg16, (((1,), (1,)), ((), ())), preferred_element_type=F32)
        for pr in range(hpg // 2):
            h0 = g * hpg + 2 * pr
            ms = []
            for h in (h0, h0 + 1):
                seg = acum[:, h:h + 1] - acum_t[h:h + 1, :]
                dec = jnp.exp(jnp.where(causal, seg, -jnp.inf))
                ms.append((cb * dec).astype(BF16))
            lhs = jnp.concatenate(ms, axis=1)
            xp = xd[:, h0 * SSM_HEAD_DIM:(h0 + 2) * SSM_HEAD_DIM]
            rhs = jnp.concatenate([jnp.where(half_lane, xp, 0.0), jnp.where(half_lane, 0.0, xp)],
                                  axis=0).astype(BF16)
            y_s[:, h0 * SSM_HEAD_DIM:(h0 + 2) * SSM_HEAD_DIM] = jnp.dot(
                lhs, rhs, preferred_element_type=F32)
        sl = slice(g * gw, (g + 1) * gw)
        ht_g = ht_s[:, sl]
        y_off = jnp.dot(cg16, ht_g.astype(BF16), preferred_element_type=F32) * ea_e[:, sl]
        y_s[:, sl] = y_s[:, sl] + y_off
        st = jnp.dot(bg.T.astype(BF16), (xd[:, sl] * de_e[:, sl]).astype(BF16), preferred_element_type=F32)
        ht_s[:, sl] = ht_g * cd_e[:, sl] + st

    y = y_s[...] + dsk_ref[...] * xs
    y = y * jax.nn.silu(z)
    nw = nw_ref[...]
    for g in range(SSM_GROUPS):
        sl = slice(g * gw, (g + 1) * gw)
        yg = y[:, sl]
        ms = jnp.sum(yg * yg, axis=1, keepdims=True) * (1.0 / gw)
        y_ref[:, sl] = (yg * lax.rsqrt(ms + RMS_EPS) * nw[:, sl])[0:rows, :].astype(y_ref.dtype)

    @pl.when(ci == nc - 1)
    def _():
        hfin_ref[0] = ht_s[...].T


def _ssd(xbc, z, small, conv_state8, h0, lw, *, row0, batch, seqlen):
    q = SSM_CHUNK
    rows = q if seqlen % q == 0 else seqlen
    assert rows % SUBLANES == 0 and rows <= q and row0 % rows == 0
    nc = seqlen // rows
    blk0 = row0 // rows
    tok = lambda b, c: (blk0 + b * nc + c, 0)
    out_tok = lambda b, c: (b * nc + c, 0)
    const2 = lambda b, c: (0, 0)
    n = batch * seqlen
    return pl.pallas_call(
        functools.partial(_ssd_body, rows=rows, nc=nc),
        grid=(batch, nc),
        in_specs=[pl.BlockSpec((rows, SSM_CONV_DIM), tok),
                  pl.BlockSpec((rows, SSM_D_INNER), tok),
                  pl.BlockSpec((rows, LANES), tok),
                  pl.BlockSpec((1, SUBLANES, SSM_CONV_DIM), lambda b, c: (b, 0, 0)),
                  pl.BlockSpec((1, SSM_D_INNER, SSM_D_STATE), lambda b, c: (b, 0, 0)),
                  pl.BlockSpec((SUBLANES, SSM_CONV_DIM), const2),
                  pl.BlockSpec((1, SSM_CONV_DIM), const2),
                  pl.BlockSpec((1, LANES), const2),
                  pl.BlockSpec((1, LANES), const2),
                  pl.BlockSpec((1, SSM_D_INNER), const2),
                  pl.BlockSpec((1, SSM_D_INNER), const2),
                  pl.BlockSpec((3 * LANES, SSM_D_INNER), const2),
                  pl.BlockSpec((q, q), const2)],
        out_specs=[pl.BlockSpec((rows, SSM_D_INNER), out_tok),
                   pl.BlockSpec((rows, LANES), out_tok),
                   pl.BlockSpec((rows, LANES), out_tok),
                   pl.BlockSpec((1, SSM_D_INNER, SSM_D_STATE), lambda b, c: (b, 0, 0))],
        out_shape=[jax.ShapeDtypeStruct((n, SSM_D_INNER), BF16),
                   jax.ShapeDtypeStruct((n, LANES), F32),
                   jax.ShapeDtypeStruct((n, LANES), F32),
                   jax.ShapeDtypeStruct((batch, SSM_D_INNER, SSM_D_STATE), F32)],
        scratch_shapes=[pltpu.VMEM((q + SUBLANES, SSM_CONV_DIM), F32),
                        pltpu.VMEM((SSM_D_STATE, SSM_D_INNER), F32),
                        pltpu.VMEM((1, LANES), F32),
                        pltpu.VMEM((q, SSM_D_INNER), F32)],
        compiler_params=_params(("parallel", "arbitrary")),
        name="ssd_scan",
    )(xbc, z, small, conv_state8, h0, lw["conv_w8"], lw["conv_b"], lw["bias128"], lw["aneg128"],
      lw["dskip_e"], lw["norm_w"], lw["expand"], lw["tri"])


FLASH_Q_STRIP = 256
FLASH_K_TILE = 128


def _flash_body(qt_ref, k_ref, vt_ref, c_ref, o_ref, m_s, l_s, acc_s, *, tq):
    qi = pl.program_id(1)
    ki = pl.program_id(2)
    n_strip = tq // FLASH_Q_STRIP
    n_kt = tq // FLASH_K_TILE

    @pl.when(ki == 0)
    def _():
        m_s[...] = jnp.full_like(m_s, -jnp.inf)
        l_s[...] = jnp.zeros_like(l_s)
        acc_s[...] = jnp.zeros_like(acc_s)

    def block(diagonal):
        for h in range(FOX_HEADS):
            hs = slice(h * FOX_HEAD_DIM, (h + 1) * FOX_HEAD_DIM)
            ck = c_ref[:, h:h + 1] * LOG2E
            for st in range(n_strip):
                cs = slice(st * FLASH_Q_STRIP, (st + 1) * FLASH_Q_STRIP)
                q_t = qt_ref[hs, cs]
                m_prev = m_s[h:h + 1, cs]
                l_prev = l_s[h:h + 1, cs]
                for kt in range(n_kt):
                    k_lo, k_hi = kt * FLASH_K_TILE, (kt + 1) * FLASH_K_TILE
                    q_lo, q_hi = st * FLASH_Q_STRIP, (st + 1) * FLASH_Q_STRIP
                    if diagonal and k_lo > q_hi - 1:
                        continue
                    ks = slice(k_lo, k_hi)
                    s = jnp.dot(k_ref[ks, hs], q_t, preferred_element_type=F32)
                    s = s - ck[ks, :]
                    if diagonal and k_hi - 1 > q_lo:
                        key_i = k_lo + lax.broadcasted_iota(jnp.int32, s.shape, 0)
                        qry_i = q_lo + lax.broadcasted_iota(jnp.int32, s.shape, 1)
                        s = jnp.where(key_i <= qry_i, s, -jnp.inf)
                    m_new = jnp.maximum(m_prev, jnp.max(s, axis=0, keepdims=True))
                    alpha = jnp.exp2(m_prev - m_new)
                    p = jnp.exp2(s - m_new)
                    l_prev = alpha * l_prev + jnp.sum(p, axis=0, keepdims=True)
                    acc_s[hs, cs] = acc_s[hs, cs] * alpha + jnp.dot(vt_ref[hs, ks], p.astype(BF16),
                                                                    preferred_element_type=F32)
                    m_prev = m_new
                m_s[h:h + 1, cs] = m_prev
                l_s[h:h + 1, cs] = l_prev

    @pl.when(ki < qi)
    def _():
        block(False)

    @pl.when(ki == qi)
    def _():
        block(True)
        for h in range(FOX_HEADS):
            hs = slice(h * FOX_HEAD_DIM, (h + 1) * FOX_HEAD_DIM)
            o_ref[:, hs] = (acc_s[hs, :] / l_s[h:h + 1, :]).T.astype(o_ref.dtype)


def _fox_prompt(q_t, k16, v_t, c, *, batch, seqlen, tq_target=512):
    tq = _pick_block(seqlen, tq_target, FLASH_Q_STRIP)
    assert tq % FLASH_Q_STRIP == 0 and tq % FLASH_K_TILE == 0
    nq = seqlen // tq
    kv_blk = lambda b, i, j: b * nq + jnp.minimum(i, j)
    return pl.pallas_call(
        functools.partial(_flash_body, tq=tq),
        grid=(batch, nq, nq),
        in_specs=[pl.BlockSpec((FOX_WIDTH, tq), lambda b, i, j: (0, b * nq + i)),
                  pl.BlockSpec((tq, FOX_WIDTH), lambda b, i, j: (kv_blk(b, i, j), 0)),
                  pl.BlockSpec((FOX_WIDTH, tq), lambda b, i, j: (0, kv_blk(b, i, j))),
                  pl.BlockSpec((tq, FOX_HEADS), lambda b, i, j: (kv_blk(b, i, j), 0))],
        out_specs=pl.BlockSpec((tq, FOX_WIDTH), lambda b, i, j: (b * nq + i, 0)),
        out_shape=jax.ShapeDtypeStruct((batch * seqlen, FOX_WIDTH), BF16),
        scratch_shapes=[pltpu.VMEM((FOX_HEADS, tq), F32), pltpu.VMEM((FOX_HEADS, tq), F32),
                        pltpu.VMEM((FOX_WIDTH, tq), F32)],
        compiler_params=_params(("parallel", "parallel", "arbitrary")),
        name="fox_prompt",
    )(q_t, k16, v_t, c)


PAGES_PER_STEP = 8


def _paged_body(pt_ref, qbd_ref, knew_ref, vnew_ref, cnew_t_ref, *rest, n_new, n_pages, pps):
    del pt_ref
    kp_refs, vp_refs, lf_refs = rest[0:pps], rest[pps:2 * pps], rest[2 * pps:3 * pps]
    o_ref, m_s, l_s, acc_s, carry_s = rest[3 * pps:]
    step = pl.program_id(1)
    page = kp_refs[0].shape[2] // FOX_HEADS
    nrow = FOX_HEADS * n_new

    def rows_of_heads(x8):
        return jnp.concatenate([jnp.broadcast_to(x8[h:h + 1, :], (n_new, x8.shape[1]))
                                for h in range(FOX_HEADS)], axis=0)

    def update(k16, v16, key_bias, mask):
        s = lax.dot_general(qbd_ref[0], k16, (((1,), (1,)), ((), ())), preferred_element_type=F32)
        s = s + rows_of_heads(key_bias * LOG2E)
        if mask is not None:
            s = jnp.where(mask, s, -jnp.inf)
        m_prev = m_s[...]
        m_new = jnp.maximum(m_prev, jnp.max(s, axis=1, keepdims=True))
        alpha = jnp.exp2(m_prev - m_new)
        p = jnp.exp2(s - m_new)
        l_s[...] = alpha * l_s[...] + jnp.sum(p, axis=1, keepdims=True)
        m_s[...] = m_new
        acc_s[...] = acc_s[...] * alpha + jnp.dot(p.astype(BF16), v16, preferred_element_type=F32)

    def head_major(ref):
        return jnp.concatenate([ref[0, 0, pl.ds(h, page, stride=FOX_HEADS), :].astype(BF16)
                                for h in range(FOX_HEADS)], axis=1)

    @pl.when(step == 0)
    def _():
        m_s[...] = jnp.full_like(m_s, -jnp.inf)
        l_s[...] = jnp.zeros_like(l_s)
        acc_s[...] = jnp.zeros_like(acc_s)
        carry_s[...] = jnp.zeros_like(carry_s)
        rr = lax.broadcasted_iota(jnp.int32, (nrow, page), 0)
        kk = lax.broadcasted_iota(jnp.int32, (nrow, page), 1)
        update(knew_ref[0].astype(BF16), vnew_ref[0].astype(BF16), -cnew_t_ref[0], kk <= rr % n_new)

    @pl.when(step > 0)
    def _():
        ii = lax.broadcasted_iota(jnp.int32, (page, page), 0)
        jj = lax.broadcasted_iota(jnp.int32, (page, page), 1)
        later = jnp.where(ii > jj, 1.0, 0.0)
        carry = carry_s[...]
        biases = []
        for r in range(pps):
            lf_t = lf_refs[r][0, 0]
            biases.append(jnp.dot(lf_t, later, precision=HIGHEST, preferred_element_type=F32) + carry)
            carry = carry + jnp.sum(lf_t, axis=1, keepdims=True)
        carry_s[...] = carry
        update(jnp.concatenate([head_major(r) for r in kp_refs], axis=0),
               jnp.concatenate([head_major(r) for r in vp_refs], axis=0),
               jnp.concatenate(biases, axis=1), None)

    @pl.when(step == pl.num_programs(1) - 1)
    def _():
        for h in range(FOX_HEADS):
            rs = slice(h * n_new, (h + 1) * n_new)
            cs = slice(h * FOX_HEAD_DIM, (h + 1) * FOX_HEAD_DIM)
            o_ref[0, :, cs] = (acc_s[rs, cs] / l_s[rs, :]).astype(o_ref.dtype)


def _fox_sample(qbd, knew, vnew, cnew_t, cache_k, cache_v, cache_lf_t, page_table, *, layer, n_new):
    bsz, n_pages = page_table.shape
    page = cache_k.shape[2] // FOX_HEADS
    nrow = qbd.shape[1]
    pps = max(d for d in range(1, PAGES_PER_STEP + 1) if n_pages % d == 0)
    n_steps = n_pages // pps

    def page_of(r):
        return lambda b, s, pt: pt[b, n_pages - 1 - ((jnp.maximum(s, 1) - 1) * pps + r)]

    kv_maps = [(lambda b, s, pt, f=page_of(r): (layer, f(b, s, pt), 0, 0)) for r in range(pps)]
    lf_maps = kv_maps
    per_b = lambda b, s, pt: (b, 0, 0)
    kv_block = (1, 1, page * FOX_HEADS, FOX_HEAD_DIM)
    grid_spec = pltpu.PrefetchScalarGridSpec(
        num_scalar_prefetch=1,
        grid=(bsz, n_steps + 1),
        in_specs=[pl.BlockSpec((1, nrow, FOX_WIDTH), per_b),
                  pl.BlockSpec((1, page, FOX_WIDTH), per_b),
                  pl.BlockSpec((1, page, FOX_WIDTH), per_b),
                  pl.BlockSpec((1, FOX_HEADS, page), per_b)]
                 + [pl.BlockSpec(kv_block, m) for m in kv_maps]
                 + [pl.BlockSpec(kv_block, m) for m in kv_maps]
                 + [pl.BlockSpec((1, 1, FOX_HEADS, page), m) for m in lf_maps],
        out_specs=pl.BlockSpec((1, n_new, FOX_WIDTH), per_b),
        scratch_shapes=[pltpu.VMEM((nrow, 1), F32), pltpu.VMEM((nrow, 1), F32),
                        pltpu.VMEM((nrow, FOX_WIDTH), F32), pltpu.VMEM((FOX_HEADS, 1), F32)])
    return pl.pallas_call(
        functools.partial(_paged_body, n_new=n_new, n_pages=n_pages, pps=pps),
        grid_spec=grid_spec,
        out_shape=jax.ShapeDtypeStruct((bsz, n_new, FOX_WIDTH), BF16),
        compiler_params=_params(("parallel", "arbitrary")),
        name="fox_sample",
    )(page_table, qbd, knew, vnew, cnew_t, *([cache_k] * pps), *([cache_v] * pps), *([cache_lf_t] * pps))


def _layer_norm(r, g, b):
    mu = jnp.mean(r, axis=1, keepdims=True)
    d = r - mu
    var = jnp.mean(d * d, axis=1, keepdims=True)
    return d * lax.rsqrt(var + LN_EPS) * g + b


def _merge_body(x_ref, ya_ref, yb_ref, ga_ref, gb_ref, wa_ref, wb_ref, wo_ref, g_ref, b_ref, o_ref, o16_ref,
                *, alpha):
    a = jnp.dot(ya_ref[...], wa_ref[...], preferred_element_type=F32)
    b = jnp.dot(yb_ref[...], wb_ref[...], preferred_element_type=F32)
    merged = jax.nn.sigmoid(ga_ref[...]) * a + jax.nn.sigmoid(gb_ref[...]) * b
    r = alpha * x_ref[...] + jnp.dot(merged.astype(BF16), wo_ref[...], preferred_element_type=F32)
    y = _layer_norm(r, g_ref[...], b_ref[...])
    o_ref[...] = y
    o16_ref[...] = y.astype(BF16)


def _merge(x, ya, yb, ga, gb, lw, alpha, tm_target=256):
    m = x.shape[0]
    tm = _pick_block(m, tm_target)
    tok = lambda i: (i, 0)
    const = lambda i: (0, 0)
    return pl.pallas_call(
        functools.partial(_merge_body, alpha=alpha),
        grid=(m // tm,),
        in_specs=[pl.BlockSpec((tm, D_MODEL), tok), pl.BlockSpec((tm, SSM_D_INNER), tok),
                  pl.BlockSpec((tm, FOX_WIDTH), tok), pl.BlockSpec((tm, D_MODEL), tok),
                  pl.BlockSpec((tm, D_MODEL), tok),
                  pl.BlockSpec((SSM_D_INNER, D_MODEL), const), pl.BlockSpec((FOX_WIDTH, D_MODEL), const),
                  pl.BlockSpec((D_MODEL, D_MODEL), const),
                  pl.BlockSpec((1, D_MODEL), const), pl.BlockSpec((1, D_MODEL), const)],
        out_specs=[pl.BlockSpec((tm, D_MODEL), tok), pl.BlockSpec((tm, D_MODEL), tok)],
        out_shape=[jax.ShapeDtypeStruct((m, D_MODEL), F32), jax.ShapeDtypeStruct((m, D_MODEL), BF16)],
        compiler_params=_params(("parallel",)),
        name="merge_ln1",
    )(x, ya, yb, ga, gb, lw["w_a"], lw["w_b"], lw["w_out"], lw["ln1_g"], lw["ln1_b"])


def _top16_by_rows(s, break_ties):
    n = s.shape[0]
    idx = lax.broadcasted_iota(jnp.int32, s.shape, 0).astype(F32)
    k_i = lax.broadcasted_iota(jnp.int32, (PEER_TOPK, s.shape[1]), 0)
    rank = jnp.full(s.shape, NOT_RANKED, F32)
    vals = jnp.zeros((PEER_TOPK, s.shape[1]), F32)
    work = s
    for k in range(PEER_TOPK):
        m = jnp.max(work, axis=0, keepdims=True)
        sel = work == m
        if break_ties:
            first = jnp.min(jnp.where(sel, idx, float(n)), axis=0, keepdims=True)
            sel = idx == first
        rank = jnp.where(sel, float(k), rank)
        work = jnp.where(sel, -jnp.inf, work)
        vals = jnp.where(k_i == k, m, vals)
    return rank, vals


def _ranked_count(rank):
    return jnp.sum(jnp.where(rank < float(PEER_TOPK), 1.0, 0.0), axis=0, keepdims=True)


def _candidate_pieces(v1, v2):
    neg = -jnp.inf
    r8 = lax.broadcasted_iota(jnp.int32, (SUBLANES, v1.shape[1]), 0)
    pieces = [v1[0:1, :] + v2[0:8, :], v1[0:1, :] + v2[8:16, :], v1[1:2, :] + v2[0:8, :]]
    for a in range(2, 8):
        nb = PEER_TOPK // (a + 1)
        pieces.append(jnp.where(r8 < nb, v1[a:a + 1, :] + v2[0:8, :], neg))
    pieces.append(v1[8:16, :] + v2[0:1, :])
    return jnp.concatenate(pieces, axis=0)


def _peer_keys_body(x_ref, wq_ref, k1_ref, k2_ref, a_ref, cut_ref, b_ref, r2_ref, s_s):
    qp = jnp.dot(x_ref[...].astype(BF16), wq_ref[...], preferred_element_type=F32)
    nt = (((1,), (1,)), ((), ()))
    for h in range(PEER_HEADS):
        q1 = qp[:, (2 * h) * PEER_HALF:(2 * h + 1) * PEER_HALF].astype(BF16)
        q2 = qp[:, (2 * h + 1) * PEER_HALF:(2 * h + 2) * PEER_HALF].astype(BF16)
        s_s[0, h] = lax.dot_general(k1_ref[h], q1, nt, preferred_element_type=F32)
        s_s[1, h] = lax.dot_general(k2_ref[h], q2, nt, preferred_element_type=F32)

    clean = None
    for h in range(PEER_HEADS):
        ok = _retrieve_head(s_s[0, h], s_s[1, h], h, a_ref, cut_ref, b_ref, r2_ref, break_ties=False)
        clean = ok if clean is None else jnp.minimum(clean, ok)

    @pl.when(jnp.min(clean) < 1.0)
    def _():
        for h in range(PEER_HEADS):
            _retrieve_head(s_s[0, h], s_s[1, h], h, a_ref, cut_ref, b_ref, r2_ref, break_ties=True)


def _retrieve_head(s1, s2, h, a_ref, cut_ref, b_ref, r2_ref, *, break_ties):
    r1, v1 = _top16_by_rows(s1, break_ties)
    r2, v2 = _top16_by_rows(s2, break_ties)
    cand = _candidate_pieces(v1, v2)
    rc, _ = _top16_by_rows(cand, break_ties)
    chosen = rc < float(PEER_TOPK)
    top = cand[0:1, :]
    denom = jnp.sum(jnp.where(chosen, jnp.exp(cand - top), 0.0), axis=0, keepdims=True)
    cnt = jnp.where(chosen, 1.0, 0.0)
    n_of_a = [jnp.sum(cnt[0:16, :], axis=0, keepdims=True)]
    for a in range(1, 8):
        n_of_a.append(jnp.sum(cnt[8 + 8 * a:16 + 8 * a, :], axis=0, keepdims=True))
    for a in range(8, 16):
        n_of_a.append(cnt[64 + a:65 + a, :])
    cut = jnp.zeros_like(r1)
    for a in range(PEER_TOPK):
        cut = jnp.where(r1 == float(a), n_of_a[a], cut)
    a_ref[h] = jnp.exp(s1 - v1[0:1, :]) / denom
    cut_ref[h] = cut
    b_ref[h] = jnp.exp(s2 - v2[0:1, :]).astype(BF16)
    r2_ref[h] = r2.astype(BF16)
    total = float(PEER_TOPK)
    ok = (_ranked_count(r1) == total) & (_ranked_count(r2) == total) & (_ranked_count(rc) == total)
    return jnp.where(ok, 1.0, 0.0)


def _peer_keys(x, lw, tm_target=256):
    m = x.shape[0]
    tm = _pick_block(m, tm_target, LANES)
    spec3 = pl.BlockSpec((PEER_HEADS, PEER_N_KEYS, tm), lambda i: (0, 0, i))
    shape3 = jax.ShapeDtypeStruct((PEER_HEADS, PEER_N_KEYS, m), F32)
    return pl.pallas_call(
        _peer_keys_body,
        grid=(m // tm,),
        in_specs=[pl.BlockSpec((tm, D_MODEL), lambda i: (i, 0)),
                  pl.BlockSpec((D_MODEL, 2 * PEER_HEADS * PEER_HALF), lambda i: (0, 0)),
                  pl.BlockSpec((PEER_HEADS, PEER_N_KEYS, PEER_HALF), lambda i: (0, 0, 0)),
                  pl.BlockSpec((PEER_HEADS, PEER_N_KEYS, PEER_HALF), lambda i: (0, 0, 0))],
        out_specs=[spec3, spec3, spec3, spec3],
        out_shape=[shape3, shape3, jax.ShapeDtypeStruct(shape3.shape, BF16), jax.ShapeDtypeStruct(shape3.shape, BF16)],
        scratch_shapes=[pltpu.VMEM((2, PEER_HEADS, PEER_N_KEYS, tm), F32)],
        compiler_params=_params(("parallel",)),
        name="peer_keys",
    )(x, lw["peer_wq"], lw["peer_k1"], lw["peer_k2"])


PEER_SUB = 256
PEER_ACC = 512
PEER_STEP = 1024
PEER_STRIP = 256


def _peer_experts_body(x_ref, u_ref, vt_ref, a_ref, cut_ref, b_ref, r2_ref, o_ref, acc_s, gate_s, coef_s):
    j = pl.program_id(1)
    tm = x_ref.shape[0]

    @pl.when(j == 0)
    def _():
        acc_s[...] = jnp.zeros_like(acc_s)

    for qq in range(PEER_STEP // PEER_N_KEYS):
        i1 = j * (PEER_STEP // PEER_N_KEYS) + qq
        rs = slice(qq * PEER_N_KEYS, (qq + 1) * PEER_N_KEYS)
        for st in range(tm // PEER_STRIP):
            cs = slice(st * PEER_STRIP, (st + 1) * PEER_STRIP)
            gate = None
            for h in range(PEER_HEADS):
                a = jnp.tile(jnp.broadcast_to(a_ref[h, pl.ds(i1, 1), cs], (16, PEER_STRIP)).astype(BF16), (8, 1))
                cut = jnp.tile(jnp.broadcast_to(cut_ref[h, pl.ds(i1, 1), cs], (16, PEER_STRIP)).astype(BF16), (8, 1))
                term = jnp.where(r2_ref[h, :, cs] < cut, b_ref[h, :, cs], jnp.zeros((), BF16)) * a
                gate = term if gate is None else gate + term
            gate_s[rs, cs] = gate.astype(F32)

    x16 = x_ref[...]
    for sb in range(PEER_STEP // PEER_SUB):
        rs = slice(sb * PEER_SUB, (sb + 1) * PEER_SUB)
        hid = lax.dot_general(u_ref[rs, :], x16, (((1,), (1,)), ((), ())), preferred_element_type=F32)
        act = 0.5 * hid * (1.0 + lax.erf(hid * (1.0 / math.sqrt(2.0))))
        coef_s[rs, :] = (gate_s[rs, :] * act).astype(BF16)
    for ab in range(PEER_STEP // PEER_ACC):
        rs = slice(ab * PEER_ACC, (ab + 1) * PEER_ACC)
        acc_s[...] = acc_s[...] + jnp.dot(vt_ref[:, rs], coef_s[rs, :], preferred_element_type=F32)

    @pl.when(j == pl.num_programs(1) - 1)
    def _():
        o_ref[...] = acc_s[...].T


def _peer_experts(x16, a, cut, b, r2, lw, tm_target=512):
    m = x16.shape[0]
    tm = _pick_block(m, tm_target, PEER_STRIP)
    assert tm % PEER_STRIP == 0
    n_exp = lw["peer_u"].shape[0]
    spec3 = pl.BlockSpec((PEER_HEADS, PEER_N_KEYS, tm), lambda i, j: (0, 0, i))
    return pl.pallas_call(
        _peer_experts_body,
        grid=(m // tm, n_exp // PEER_STEP),
        in_specs=[pl.BlockSpec((tm, D_MODEL), lambda i, j: (i, 0)),
                  pl.BlockSpec((PEER_STEP, D_MODEL), lambda i, j: (j, 0)),
                  pl.BlockSpec((D_MODEL, PEER_STEP), lambda i, j: (0, j)),
                  spec3, spec3, spec3, spec3],
        out_specs=pl.BlockSpec((tm, D_MODEL), lambda i, j: (i, 0)),
        out_shape=jax.ShapeDtypeStruct((m, D_MODEL), F32),
        scratch_shapes=[pltpu.VMEM((D_MODEL, tm), F32), pltpu.VMEM((PEER_STEP, tm), F32),
                        pltpu.VMEM((PEER_STEP, tm), BF16)],
        compiler_params=_params(("parallel", "arbitrary")),
        name="peer_experts",
    )(x16, lw["peer_u"], lw["peer_vt"], a, cut, b, r2)


def _post_body(x_ref, f_ref, p_ref, g_ref, b_ref, wg_ref, wp_ref, o_ref, *, alpha):
    x2 = _layer_norm(alpha * x_ref[...] + f_ref[...], g_ref[...], b_ref[...])
    gate = jax.nn.sigmoid(jnp.dot(x2.astype(BF16), wg_ref[...], preferred_element_type=F32))
    emb = jnp.dot(p_ref[...].astype(BF16), wp_ref[...], preferred_element_type=F32)
    o_ref[...] = x2 + gate * emb


def _post(x, ffn, p_emb, lw, alpha, row0=0, nrows=None, tm_target=512):
    m = x.shape[0] if nrows is None else nrows
    tm = _pick_block(math.gcd(m, row0) if row0 else m, tm_target)
    blk0 = row0 // tm
    ple = p_emb.shape[1]
    tok = lambda i: (blk0 + i, 0)
    const = lambda i: (0, 0)
    return pl.pallas_call(
        functools.partial(_post_body, alpha=alpha),
        grid=(m // tm,),
        in_specs=[pl.BlockSpec((tm, D_MODEL), tok), pl.BlockSpec((tm, D_MODEL), tok),
                  pl.BlockSpec((tm, ple), tok),
                  pl.BlockSpec((1, D_MODEL), const), pl.BlockSpec((1, D_MODEL), const),
                  pl.BlockSpec((D_MODEL, D_MODEL), const), pl.BlockSpec((ple, D_MODEL), const)],
        out_specs=pl.BlockSpec((tm, D_MODEL), lambda i: (i, 0)),
        out_shape=jax.ShapeDtypeStruct((m, D_MODEL), F32),
        compiler_params=_params(("parallel",)),
        name="ln2_ple",
    )(x, ffn, p_emb, lw["ln2_g"], lw["ln2_b"], lw["w_ple_gate"], lw["w_ple_proj"])


def _layer_weights(i, w_in, conv_w, conv_b, dt_bias, a_log, d_skip, ssm_norm_w, fox_f_bias, w_branch_a,
                   w_branch_b, w_out, ln1_g, ln1_b, ln2_g, ln2_b, peer_w_q, peer_sub_keys, peer_u, peer_v,
                   w_ple_gate, w_ple_proj):
    sizes = (SSM_D_INNER, SSM_CONV_DIM, SSM_HEADS, FOX_WIDTH, FOX_WIDTH, FOX_WIDTH, FOX_HEADS, D_MODEL, D_MODEL)
    offs = np.concatenate([[0], np.cumsum(sizes)])
    w = w_in[i]
    cols = {n: w[:, offs[j]:offs[j + 1]] for j, n in enumerate(("z", "xbc", "dt", "q", "k", "v", "f", "ga", "gb"))}
    small = jnp.concatenate([cols["dt"], cols["f"],
                             jnp.zeros((D_MODEL, LANES - SSM_HEADS - FOX_HEADS), F32)], axis=1)
    zpad = jnp.zeros((LANES - SSM_HEADS - FOX_HEADS,), F32)
    lw = {n: cols[n].astype(BF16) for n in ("z", "xbc", "k", "v", "ga", "gb")}
    lw["q_t"] = cols["q"].T.astype(BF16)
    lw["v_t"] = cols["v"].T.astype(BF16)
    lw["small"] = small.astype(BF16)
    lw["conv_w8"] = jnp.concatenate([conv_w[i], jnp.zeros((SUBLANES - SSM_CONV_W, SSM_CONV_DIM), F32)], axis=0)
    lw["conv_b"] = conv_b[i][None, :]
    lw["bias128"] = jnp.concatenate([dt_bias[i], fox_f_bias[i], zpad])[None, :]
    lw["aneg128"] = jnp.concatenate([-jnp.exp(a_log[i]), jnp.zeros((LANES - SSM_HEADS,), F32)])[None, :]
    lw["dskip_e"] = jnp.repeat(d_skip[i], SSM_HEAD_DIM)[None, :]
    lw["norm_w"] = ssm_norm_w[i][None, :]
    head_of_col = np.arange(SSM_D_INNER) // SSM_HEAD_DIM
    expand = np.arange(LANES)[:, None] == head_of_col[None, :]
    lw["expand"] = jnp.asarray(np.concatenate([expand] * 3, axis=0), BF16)
    lw["tri"] = jnp.asarray(np.tril(np.ones((SSM_CHUNK, SSM_CHUNK), np.float32)))
    lw["w_a"] = w_branch_a[i].astype(BF16)
    lw["w_b"] = w_branch_b[i].astype(BF16)
    lw["w_out"] = w_out[i].astype(BF16)
    lw["ln1_g"], lw["ln1_b"] = ln1_g[i][None, :], ln1_b[i][None, :]
    lw["ln2_g"], lw["ln2_b"] = ln2_g[i][None, :], ln2_b[i][None, :]
    lw["peer_wq"] = peer_w_q[i].astype(BF16)
    lw["peer_k1"] = peer_sub_keys[i, 0].astype(BF16)
    lw["peer_k2"] = peer_sub_keys[i, 1].astype(BF16)
    lw["peer_u"] = peer_u[i].astype(BF16)
    lw["peer_vt"] = peer_v[i].T.astype(BF16)
    lw["w_ple_gate"] = w_ple_gate[i].astype(BF16)
    lw["w_ple_proj"] = w_ple_proj[i].astype(BF16)
    return lw


def _trunk_layer(x, p_emb, lw, conv_state, ssm_state, cache_k, cache_v, cache_lf_t, page_table, *, layer,
                 bp, lp, bs, ls, alpha, last):
    tp = bp * lp
    ts = bs * ls
    n_rows = x.shape[0]
    z = _matmul(x, lw["z"])
    xbc = _matmul(x, lw["xbc"])
    small = _matmul(x, lw["small"])
    q_t = _matmul_t(lw["q_t"], x, scale=FOX_SCALE * LOG2E)
    v_t = _matmul_t(lw["v_t"], x, nrows=tp)
    k_p, k16 = _matmul(x, lw["k"], out_dtypes=(F32, BF16), nrows=tp)
    v_p = _matmul(x, lw["v"], nrows=tp)
    k_s = _matmul(x, lw["k"], row0=tp, nrows=ts)
    v_s = _matmul(x, lw["v"], row0=tp, nrows=ts)
    ga = _matmul(x, lw["ga"])
    gb = _matmul(x, lw["gb"])

    zero_conv = jnp.zeros((bp, SUBLANES, SSM_CONV_DIM), F32)
    zero_h = jnp.zeros((bp, SSM_D_INNER, SSM_D_STATE), F32)
    ya_p, lf_p, c_p, h_p = _ssd(xbc, z, small, zero_conv, zero_h, lw, row0=0, batch=bp, seqlen=lp)
    conv8 = jnp.concatenate([jnp.zeros((bs, SUBLANES - (SSM_CONV_W - 1), SSM_CONV_DIM), F32), conv_state], axis=1)
    ya_s, lf_s, c_s, h_s = _ssd(xbc, z, small, conv8, ssm_state.reshape(bs, SSM_D_INNER, SSM_D_STATE), lw,
                                row0=tp, batch=bs, seqlen=ls)
    logf_p = lf_p[:, F_LO:F_HI]
    logf_s = lf_s[:, F_LO:F_HI]
    cum_p = c_p[:, F_LO:F_HI]
    cum_s = c_s[:, F_LO:F_HI].reshape(bs, ls, FOX_HEADS)

    yb_p = _fox_prompt(q_t, k16, v_t, cum_p, batch=bp, seqlen=lp)
    page = cache_lf_t.shape[3]
    qs = q_t[:, tp:tp + ts].T.reshape(bs, ls, FOX_HEADS, FOX_HEAD_DIM)
    eye = jnp.eye(FOX_HEADS, dtype=BF16)
    qbd = (qs.transpose(0, 2, 1, 3)[:, :, :, None, :] * eye[None, :, None, :, None]).reshape(
        bs, FOX_HEADS * ls, FOX_WIDTH)
    knew = jnp.pad(k_s.reshape(bs, ls, FOX_WIDTH), ((0, 0), (0, page - ls), (0, 0)))
    vnew = jnp.pad(v_s.reshape(bs, ls, FOX_WIDTH), ((0, 0), (0, page - ls), (0, 0)))
    cnew_t = jnp.pad(cum_s.transpose(0, 2, 1), ((0, 0), (0, 0), (0, page - ls)))
    yb_s = _fox_sample(qbd, knew, vnew, cnew_t, cache_k, cache_v, cache_lf_t, page_table,
                       layer=layer, n_new=ls).reshape(ts, FOX_WIDTH)

    n_pad = n_rows - tp - ts
    ya = jnp.concatenate([ya_p, ya_s, jnp.zeros((n_pad, SSM_D_INNER), BF16)], axis=0)
    yb = jnp.concatenate([yb_p, yb_s, jnp.zeros((n_pad, FOX_WIDTH), BF16)], axis=0)
    x1, x1_16 = _merge(x, ya, yb, ga, gb, lw, alpha)
    a, cut, b, r2 = _peer_keys(x1_16, lw)
    ffn = _peer_experts(x1_16, a, cut, b, r2, lw)
    if last:
        x2 = (_post(x1, ffn, p_emb, lw, alpha, nrows=tp), _post(x1, ffn, p_emb, lw, alpha, row0=tp, nrows=ts))
    else:
        x2 = _post(x1, ffn, p_emb, lw, alpha)

    hshape = (FOX_HEADS, FOX_HEAD_DIM)
    sshape = (SSM_HEADS, SSM_HEAD_DIM, SSM_D_STATE)
    xbc_p = xbc[:tp].reshape(bp, lp, SSM_CONV_DIM)
    xbc_s = xbc[tp:tp + ts].reshape(bs, ls, SSM_CONV_DIM)
    prompt_out = (k_p.reshape(bp, lp, *hshape), v_p.reshape(bp, lp, *hshape),
                  logf_p.reshape(bp, lp, FOX_HEADS), h_p.reshape(bp, *sshape),
                  xbc_p[:, lp - (SSM_CONV_W - 1):])
    sample_out = (k_s.reshape(bs, ls, *hshape), v_s.reshape(bs, ls, *hshape),
                  logf_s.reshape(bs, ls, FOX_HEADS), h_s.reshape(bs, *sshape),
                  jnp.concatenate([conv_state, xbc_s], axis=1)[:, ls:])
    return x2, prompt_out, sample_out


def kernel(x_prompt, x_sample, cache_k, cache_v, cache_logf, state_ssm, state_conv, page_table, p_prompt, p_sample, w_in, conv_w, conv_b, dt_bias, a_log, d_skip, ssm_norm_w, fox_f_bias, w_branch_a, w_branch_b, w_out, ln1_g, ln1_b, ln2_g, ln2_b, peer_w_q, peer_sub_keys, peer_u, peer_v, w_ple_gate, w_ple_proj):
    depth = w_in.shape[0]
    bp, lp, _ = x_prompt.shape
    bs, ls, _ = x_sample.shape
    alpha = (2.0 * depth) ** 0.25
    clf_t = cache_logf.transpose(0, 1, 3, 2)
    n_pool, page = cache_k.shape[1], cache_k.shape[2]
    cache_k = cache_k.reshape(depth, n_pool, page * FOX_HEADS, FOX_HEAD_DIM)
    cache_v = cache_v.reshape(depth, n_pool, page * FOX_HEADS, FOX_HEAD_DIM)
    tp, ts = bp * lp, bs * ls
    n_pad = -(tp + ts) % TOKEN_TILE
    x = jnp.concatenate([x_prompt.reshape(tp, D_MODEL), x_sample.reshape(ts, D_MODEL),
                         jnp.zeros((n_pad, D_MODEL), F32)], axis=0)
    outs_p, outs_s = [], []
    for i in range(depth):
        lw = _layer_weights(i, w_in, conv_w, conv_b, dt_bias, a_log, d_skip, ssm_norm_w, fox_f_bias, w_branch_a,
                            w_branch_b, w_out, ln1_g, ln1_b, ln2_g, ln2_b, peer_w_q, peer_sub_keys, peer_u,
                            peer_v, w_ple_gate, w_ple_proj)
        ple = p_prompt.shape[-1]
        p_emb = jnp.concatenate([p_prompt[i].reshape(tp, ple), p_sample[i].reshape(ts, ple),
                                 jnp.zeros((n_pad, ple), F32)], axis=0)
        x, po, so = _trunk_layer(x, p_emb, lw, state_conv[i], state_ssm[i], cache_k, cache_v, clf_t, page_table,
                                 layer=i, bp=bp, lp=lp, bs=bs, ls=ls, alpha=alpha, last=i == depth - 1)
        outs_p.append(po)
        outs_s.append(so)
    stack = lambda outs, j: jnp.stack([o[j] for o in outs])
    y_prompt, y_sample = x
    return (y_prompt.reshape(bp, lp, D_MODEL), y_sample.reshape(bs, ls, D_MODEL),
            stack(outs_p, 0), stack(outs_p, 1), stack(outs_p, 2), stack(outs_p, 3), stack(outs_p, 4),
            stack(outs_s, 0), stack(outs_s, 1), stack(outs_s, 2), stack(outs_s, 3), stack(outs_s, 4))
```
